```python
import math
import jax
import jax.numpy as jnp
from jax import lax
import numpy as np

D_MODEL = 1024
BATCH = 2
SEQ = 16384
DEPTH = 4

N_MIXERS = 2
N_A_LAYERS = (DEPTH + 1) // 2
N_B_LAYERS = DEPTH // 2
NORM_EPS = 1e-5
D_FF = 4 * D_MODEL
N_MOD = 6

S5_WIDTH = D_MODEL
S5_GROUP = 16
S5_GROUPS = S5_WIDTH // S5_GROUP
S5_STATE = 64
S5_CHUNK = 128
S5_DT_MIN = 0.001
S5_DT_MAX = 0.1

M2_D_INNER = 2 * D_MODEL
M2_HEADDIM = 64
M2_HEADS = M2_D_INNER // M2_HEADDIM
M2_GROUPS = 4
M2_HPG = M2_HEADS // M2_GROUPS
M2_STATE = 128
M2_CONV = 4
M2_CHUNK = 128
M2_CONV_DIM = M2_D_INNER + 2 * M2_GROUPS * M2_STATE
M2_IN_DIM = M2_D_INNER + M2_CONV_DIM + M2_HEADS
M2_DT_MIN = 0.001
M2_DT_MAX = 0.1

kernel_name = 'hybrid_s5_ssd_adaln_trunk'


def _rmsnorm(x, g):
    xf = x.astype(jnp.float32)
    y = xf * lax.rsqrt(jnp.mean(xf * xf, axis=-1, keepdims=True) + NORM_EPS)
    return (y * g.astype(jnp.float32)).astype(x.dtype)


def _modulate(h, shift, scale):
    return h * (1 + scale) + shift


def _sq_relu_mlp(h, w1, w2):
    a = jax.nn.relu(h @ w1)
    return (a * a) @ w2


def _s5_combine(e1, e2):
    a1, b1 = e1
    a2, b2 = e2
    return a2 * a1, a2 * b1 + b2


def _s5_mixer(h, w_in, lam_re, lam_im, log_dt, b_re, b_im, c_re, c_im, d, w_glu, b_glu):
    f32 = jnp.float32
    bsz, L, _ = h.shape
    nc = L // S5_CHUNK
    u = (h @ w_in).astype(f32)
    lam = lax.complex(lam_re.astype(f32), lam_im.astype(f32))
    dt = jnp.exp(log_dt.astype(f32))[:, None]
    lam_dt = lam * dt
    lam_bar = jnp.exp(lam_dt)
    b_bar = ((lam_bar - 1) / lam)[..., None] * lax.complex(b_re.astype(f32), b_im.astype(f32))
    c_mat = lax.complex(c_re.astype(f32), c_im.astype(f32))
    steps = jnp.arange(1, S5_CHUNK + 1, dtype=f32)[:, None, None]
    a_pow = jnp.exp(lam_dt[None] * steps)
    a_elems = jnp.broadcast_to(lam_bar, (bsz, S5_CHUNK, S5_GROUPS, S5_STATE))
    u_blocks = u.reshape(bsz, nc, S5_CHUNK, S5_GROUPS, S5_GROUP).transpose(1, 0, 2, 3, 4)

    def block_step(carry, u_blk):
        bu = jnp.einsum('btgh,gph->btgp', u_blk.astype(jnp.complex64), b_bar)
        _, hs = lax.associative_scan(_s5_combine, (a_elems, bu), axis=1)
        hs = hs + a_pow[None] * carry[:, None]
        y = jnp.einsum('btgp,ghp->btgh', hs, c_mat).real
        return hs[:, -1], y

    carry0 = jnp.zeros((bsz, S5_GROUPS, S5_STATE), jnp.complex64)
    _, ys = lax.scan(block_step, carry0, u_blocks)
    y = ys.transpose(1, 0, 2, 3, 4).reshape(bsz, L, S5_WIDTH) + d.astype(f32) * u
    g = jax.nn.gelu(y)
    ab = g @ w_glu.astype(f32) + b_glu.astype(f32)
    val, gate = jnp.split(ab, 2, axis=-1)
    return (val * jax.nn.sigmoid(gate)).astype(h.dtype)


def _causal_dwconv(u, w):
    return lax.conv_general_dilated(u, w[:, None, :].astype(u.dtype), (1,), [(M2_CONV - 1, 0)],
                                    dimension_numbers=('NWC', 'WIO', 'NWC'),
                                    feature_group_count=u.shape[-1])


def _segsum(a):
    t = a.shape[-1]
    cs = jnp.cumsum(a, axis=-1)
    diff = cs[..., :, None] - cs[..., None, :]
    mask = jnp.tril(jnp.ones((t, t), dtype=bool))
    return jnp.where(mask, diff, -jnp.inf)


def _ssd_mixer(h, w_in, conv_w, conv_b, dt_bias, a_log, d, norm_g, w_out):
    f32 = jnp.float32
    bsz, L, _ = h.shape
    nc = L // M2_CHUNK
    G, R, P, N, Q = M2_GROUPS, M2_HPG, M2_HEADDIM, M2_STATE, M2_CHUNK
    zxbcdt = h @ w_in
    z, xbc, dt_raw = jnp.split(zxbcdt, [M2_D_INNER, M2_D_INNER + M2_CONV_DIM], axis=-1)
    xbc = jax.nn.silu(_causal_dwconv(xbc, conv_w) + conv_b)
    xs, b_in, c_in = jnp.split(xbc, [M2_D_INNER, M2_D_INNER + G * N], axis=-1)
    dt = jax.nn.softplus(dt_raw.astype(f32) + dt_bias.astype(f32))
    a = -jnp.exp(a_log.astype(f32)).reshape(G, R)
    xs = xs.astype(f32).reshape(bsz, nc, Q, G, R, P)
    b_in = b_in.astype(f32).reshape(bsz, nc, Q, G, N)
    c_in = c_in.astype(f32).reshape(bsz, nc, Q, G, N)
    dt = dt.reshape(bsz, nc, Q, G, R)
    xdt = xs * dt[..., None]
    a_dt = jnp.transpose(dt * a, (0, 3, 4, 1, 2))
    a_cs = jnp.cumsum(a_dt, axis=-1)
    l_mat = jnp.exp(_segsum(a_dt))
    cb = jnp.einsum('bclgn,bcsgn->bcgls', c_in, b_in)
    y_diag = jnp.einsum('bcgls,bgrcls,bcsgrp->bclgrp', cb, l_mat, xdt)
    decay_states = jnp.exp(a_cs[..., -1:] - a_cs)
    states = jnp.einsum('bclgn,bgrcl,bclgrp->bcgrpn', b_in, decay_states, xdt)
    chunk_decay = jnp.exp(a_cs[..., -1])

    def carry_step(carry, inp):
        st, dec = inp
        return dec[..., None, None] * carry + st, carry

    _, prev = lax.scan(carry_step, jnp.zeros((bsz, G, R, P, N), f32),
                       (jnp.moveaxis(states, 1, 0), jnp.moveaxis(chunk_decay, -1, 0)))
    prev = jnp.moveaxis(prev, 0, 1)
    y_off = jnp.einsum('bclgn,bcgrpn,bgrcl->bclgrp', c_in, prev, jnp.exp(a_cs))
    y = y_diag + y_off + xs * d.astype(f32).reshape(G, R, 1)
    y = y.reshape(bsz, L, M2_D_INNER) * jax.nn.silu(z.astype(f32))
    yg = y.reshape(bsz, L, G, M2_D_INNER // G)
    yg = yg * lax.rsqrt(jnp.mean(yg * yg, axis=-1, keepdims=True) + NORM_EPS)
    y = yg.reshape(bsz, L, M2_D_INNER) * norm_g.astype(f32)
    return y.astype(h.dtype) @ w_out


def setup_inputs(seed: int = 0) -> dict:
    f32 = jnp.float32
    key = jax.random.key(seed)
    it = iter(jax.random.split(key, 28))

    def nrm(shape, scale):
        return scale * jax.random.normal(next(it), shape, f32)

    D, F = D_MODEL, D_FF
    NA, NB = N_A_LAYERS, N_B_LAYERS
    G, P, H = S5_GROUPS, S5_STATE, S5_GROUP
    x = nrm((BATCH, SEQ, D), 1.0)
    c = nrm((BATCH, D), 1.0)
    ada_w = nrm((DEPTH, D, N_MOD * D), 0.5 * D ** -0.5)
    ada_b = nrm((DEPTH, N_MOD * D), 0.02)
    norm_mix_g = 1.0 + nrm((DEPTH, D), 0.05)
    norm_mlp_g = 1.0 + nrm((DEPTH, D), 0.05)
    mlp_w1 = nrm((DEPTH, D, F), D ** -0.5)
    mlp_w2 = nrm((DEPTH, F, D), F ** -0.5)
    s5_w_in = nrm((NA, D, S5_WIDTH), D ** -0.5)
    s5_lambda_re = -0.5 + nrm((NA, G, P), 0.01)
    s5_lambda_im = math.pi * jnp.arange(P, dtype=f32) + nrm((NA, G, P), 0.01)
    s5_log_dt = jax.random.uniform(next(it), (NA, G), f32, minval=math.log(S5_DT_MIN), maxval=math.log(S5_DT_MAX))
    s5_b_re = nrm((NA, G, P, H), (2 * H) ** -0.5)
    s5_b_im = nrm((NA, G, P, H), (2 * H) ** -0.5)
    s5_c_re = nrm((NA, G, H, P), (2 * P) ** -0.5)
    s5_c_im = nrm((NA, G, H, P), (2 * P) ** -0.5)
    s5_d = nrm((NA, S5_WIDTH), 1.0)
    s5_w_glu = nrm((NA, S5_WIDTH, 2 * D), S5_WIDTH ** -0.5)
    s5_b_glu = nrm((NA, 2 * D), 0.02)
    m2_w_in = nrm((NB, D, M2_IN_DIM), D ** -0.5)
    m2_conv_w = nrm((NB, M2_CONV, M2_CONV_DIM), M2_CONV ** -0.5)
    m2_conv_b = nrm((NB, M2_CONV_DIM), 0.02)
    dt0 = jnp.exp(jax.random.uniform(next(it), (NB, M2_HEADS), f32, minval=math.log(M2_DT_MIN), maxval=math.log(M2_DT_MAX)))
    m2_dt_bias = dt0 + jnp.log(-jnp.expm1(-dt0))
    m2_a_log = jnp.log(jax.random.uniform(next(it), (NB, M2_HEADS), f32, minval=1.0, maxval=16.0))
    m2_d = 1.0 + nrm((NB, M2_HEADS), 0.1)
    m2_norm_g = 1.0 + nrm((NB, M2_D_INNER), 0.05)
    m2_w_out = nrm((NB, M2_D_INNER, D), M2_D_INNER ** -0.5)
    final_norm_g = 1.0 + nrm((D,), 0.05)
    return {'x': x, 'c': c, 'ada_w': ada_w, 'ada_b': ada_b,
            'norm_mix_g': norm_mix_g, 'norm_mlp_g': norm_mlp_g,
            'mlp_w1': mlp_w1, 'mlp_w2': mlp_w2,
            's5_w_in': s5_w_in, 's5_lambda_re': s5_lambda_re, 's5_lambda_im': s5_lambda_im,
            's5_log_dt': s5_log_dt, 's5_b_re': s5_b_re, 's5_b_im': s5_b_im,
            's5_c_re': s5_c_re, 's5_c_im': s5_c_im, 's5_d': s5_d,
            's5_w_glu': s5_w_glu, 's5_b_glu': s5_b_glu,
            'm2_w_in': m2_w_in, 'm2_conv_w': m2_conv_w, 'm2_conv_b': m2_conv_b,
            'm2_dt_bias': m2_dt_bias, 'm2_a_log': m2_a_log, 'm2_d': m2_d,
            'm2_norm_g': m2_norm_g, 'm2_w_out': m2_w_out,
            'final_norm_g': final_norm_g}


def reference(x, c, ada_w, ada_b, norm_mix_g, norm_mlp_g, mlp_w1, mlp_w2,
              s5_w_in, s5_lambda_re, s5_lambda_im, s5_log_dt, s5_b_re, s5_b_im,
              s5_c_re, s5_c_im, s5_d, s5_w_glu, s5_b_glu,
              m2_w_in, m2_conv_w, m2_conv_b, m2_dt_bias, m2_a_log, m2_d,
              m2_norm_g, m2_w_out, final_norm_g):
    cond = jax.nn.silu(c)
    for i in range(DEPTH):
        mod = cond @ ada_w[i] + ada_b[i]
        sh1, sc1, g1, sh2, sc2, g2 = jnp.split(mod[:, None, :], N_MOD, axis=-1)
        h = _modulate(_rmsnorm(x, norm_mix_g[i]), sh1, sc1)
        j = i // N_MIXERS
        if i % N_MIXERS == 0:
            y = _s5_mixer(h, s5_w_in[j], s5_lambda_re[j], s5_lambda_im[j], s5_log_dt[j],
                          s5_b_re[j], s5_b_im[j], s5_c_re[j], s5_c_im[j], s5_d[j],
                          s5_w_glu[j], s5_b_glu[j])
        else:
            y = _ssd_mixer(h, m2_w_in[j], m2_conv_w[j], m2_conv_b[j], m2_dt_bias[j],
                           m2_a_log[j], m2_d[j], m2_norm_g[j], m2_w_out[j])
        x = x + g1 * y
        h = _modulate(_rmsnorm(x, norm_mlp_g[i]), sh2, sc2)
        x = x + g2 * _sq_relu_mlp(h, mlp_w1[i], mlp_w2[i])
    return _rmsnorm(x, final_norm_g)
```

```python
import functools

import jax
import jax.numpy as jnp
from jax import lax
from jax.experimental import pallas as pl
from jax.experimental.pallas import tpu as pltpu

F32 = jnp.float32
BF16 = jnp.bfloat16
HIGHEST = lax.Precision.HIGHEST
NORM_EPS = 1e-5
N_MOD = 6

V7X_LANES = 128
V7X_SUBLANES = 8
V7X_SCOPED_VMEM_BYTES = 60000 * 1024

S5_GROUP = 16
S5_STATE = 64
S5_Q = 16
S5_QW = S5_Q * S5_GROUP
S5_PAIR_STATE = 4 * S5_STATE

M2_HEADDIM = 64
M2_GROUPS = 4
M2_STATE = 128
M2_CONV = 4
M2_CHUNK = 128
M2_HEAD_PAIR = 2 * M2_HEADDIM

TOKEN_BLOCK = 512
SCAN_BLOCK = 64


def _compiler_params(semantics, block_bytes, temp_bytes):
    want = 2 * block_bytes + temp_bytes
    return pltpu.CompilerParams(
        dimension_semantics=semantics,
        vmem_limit_bytes=int(min(V7X_SCOPED_VMEM_BYTES, max(want, 16 * 1024 * 1024))))


def _nbytes(shape, dtype):
    n = 1
    for s in shape:
        n *= s
    return n * jnp.dtype(dtype).itemsize


def _const_spec(shape):
    return pl.BlockSpec(shape, lambda *_: (0,) * len(shape), pipeline_mode=pl.Buffered(1))


def _rms_mod(x, g, shift, scale):
    y = x * lax.rsqrt(jnp.mean(x * x, axis=-1, keepdims=True) + NORM_EPS)
    return (y * g) * (1.0 + scale) + shift


def _softplus(x):
    return jnp.maximum(x, 0.0) + jnp.log1p(jnp.exp(-jnp.abs(x)))


def _ada_kernel(c_ref, w_ref, b_ref, o_ref):
    cond = jax.nn.silu(c_ref[...])
    o_ref[0] = jnp.dot(cond, w_ref[0], precision=HIGHEST,
                       preferred_element_type=F32) + b_ref[0]


def _ada_modulation(c, ada_w, ada_b):
    depth, d, n = ada_w.shape
    bsz = c.shape[0]
    rows = V7X_SUBLANES
    nb = n // 4
    c_pad = jnp.zeros((rows, d), F32).at[:bsz].set(c)
    out = pl.pallas_call(
        _ada_kernel,
        grid=(depth, n // nb),
        in_specs=[pl.BlockSpec((rows, d), lambda i, j: (0, 0)),
                  pl.BlockSpec((1, d, nb), lambda i, j: (i, 0, j)),
                  pl.BlockSpec((1, 1, nb), lambda i, j: (i, 0, j))],
        out_specs=pl.BlockSpec((1, rows, nb), lambda i, j: (i, 0, j)),
        out_shape=jax.ShapeDtypeStruct((depth, rows, n), F32),
        compiler_params=_compiler_params(("arbitrary", "arbitrary"),
                                         _nbytes((d, nb), F32), 4 * 1024 * 1024),
        name="ada_modulation",
    )(c_pad, ada_w, ada_b.reshape(depth, 1, n))
    return out[:, :bsz, :]


def _mlp_kernel(x_ref, g_ref, sh_ref, sc_ref, gate_ref, w1_ref, w2_ref, *rest, final):
    if final:
        fg_ref, o_ref = rest
    else:
        (o_ref,) = rest
    x = x_ref[0]
    h = _rms_mod(x, g_ref[...], sh_ref[0], sc_ref[0])
    a = jnp.maximum(jnp.dot(h.astype(BF16), w1_ref[...], preferred_element_type=F32), 0.0)
    y = jnp.dot((a * a).astype(BF16), w2_ref[...], preferred_element_type=F32)
    out = x + gate_ref[0] * y
    if final:
        out = out * lax.rsqrt(jnp.mean(out * out, axis=-1, keepdims=True) + NORM_EPS)
        out = out * fg_ref[...]
    o_ref[0] = out


def _mlp_layer(x, norm_g, shift, scale, gate, w1, w2, final_g=None):
    bsz, seq, d = x.shape
    f = w1.shape[1]
    tl = TOKEN_BLOCK
    tok = pl.BlockSpec((1, tl, d), lambda b, i: (b, i, 0))
    per_b = pl.BlockSpec((1, 1, d), lambda b, i: (b, 0, 0))
    in_specs = [tok, _const_spec((1, d)), per_b, per_b, per_b,
                _const_spec((d, f)), _const_spec((f, d))]
    args = [x, norm_g.reshape(1, d), shift, scale, gate, w1.astype(BF16), w2.astype(BF16)]
    if final_g is not None:
        in_specs.append(_const_spec((1, d)))
        args.append(final_g.reshape(1, d))
    return pl.pallas_call(
        functools.partial(_mlp_kernel, final=final_g is not None),
        grid=(bsz, seq // tl),
        in_specs=in_specs,
        out_specs=tok,
        out_shape=jax.ShapeDtypeStruct(x.shape, F32),
        compiler_params=_compiler_params(
            ("parallel", "parallel"),
            2 * _nbytes((tl, d), F32) + _nbytes((d, f), BF16),
            _nbytes((tl, f), F32) * 2 + _nbytes((tl, d), F32) * 2),
        name="mlp",
    )(*args)


def _s5_in_kernel(x_ref, g_ref, sh_ref, sc_ref, w_ref, u_ref):
    h = _rms_mod(x_ref[0], g_ref[...], sh_ref[0], sc_ref[0])
    u_ref[0] = jnp.dot(h.astype(BF16), w_ref[...], preferred_element_type=F32)


def _s5_state_kernel(x_ref, ws_ref, s_ref):
    s_ref[0, 0] = jnp.dot(x_ref[0, 0], ws_ref[0], preferred_element_type=F32)


def _s5_scan_kernel(a_ref, s_ref, o_ref, h_ref):
    half = a_ref.shape[1] // 2

    @pl.when(pl.program_id(0) == 0)
    def _():
        h_ref[...] = jnp.zeros_like(h_ref)

    a_re = a_ref[:, :half]
    a_im = a_ref[:, half:]

    def body(c, carry):
        h_re, h_im = carry
        o_ref[c, :, :half] = h_re
        o_ref[c, :, half:] = h_im
        s = s_ref[c]
        return (a_re * h_re - a_im * h_im + s[:, :half],
                a_re * h_im + a_im * h_re + s[:, half:])

    h_re, h_im = lax.fori_loop(0, s_ref.shape[0], body,
                               (h_ref[:, :half], h_ref[:, half:]))
    h_ref[:, :half] = h_re
    h_ref[:, half:] = h_im


def _s5_y_kernel(x_ref, h_ref, t_ref, wy_ref, y_ref):
    y = jnp.dot(x_ref[0, 0], t_ref[0], preferred_element_type=F32)
    y += jnp.dot(h_ref[0, 0].astype(BF16), wy_ref[0], preferred_element_type=F32)
    y_ref[0, 0] = y


def _s5_out_kernel(y_ref, u_ref, x_ref, d_ref, w_ref, b_ref, gate_ref, o_ref):
    d = x_ref.shape[-1]
    g = jax.nn.gelu(y_ref[0] + d_ref[...] * u_ref[0])
    ab = jnp.dot(g.astype(BF16), w_ref[...], preferred_element_type=F32) + b_ref[...]
    o_ref[0] = x_ref[0] + gate_ref[0] * (ab[:, :d] * jax.nn.sigmoid(ab[:, d:]))


def _s5_tables(lam_re, lam_im, log_dt, b_re, b_im, c_re, c_im):
    g, p = lam_re.shape
    q, hh = S5_Q, S5_GROUP
    dt = jnp.exp(log_dt)[:, None]
    ld_re, ld_im = lam_re * dt, lam_im * dt

    def lam_pow(k):
        mag = jnp.exp(ld_re[None] * k[:, None, None])
        ang = ld_im[None] * k[:, None, None]
        return mag * jnp.cos(ang), mag * jnp.sin(ang)

    lb_re, lb_im = lam_pow(jnp.ones((1,), F32))
    lb_re, lb_im = lb_re[0], lb_im[0]
    den = lam_re * lam_re + lam_im * lam_im
    f_re = ((lb_re - 1.0) * lam_re + lb_im * lam_im) / den
    f_im = (lb_im * lam_re - (lb_re - 1.0) * lam_im) / den
    bb_re = f_re[..., None] * b_re - f_im[..., None] * b_im
    bb_im = f_re[..., None] * b_im + f_im[..., None] * b_re

    steps = jnp.arange(q + 1, dtype=F32)
    pw_re, pw_im = lam_pow(steps)

    cp_re = c_re[None] * pw_re[:q, :, None, :] - c_im[None] * pw_im[:q, :, None, :]
    cp_im = c_re[None] * pw_im[:q, :, None, :] + c_im[None] * pw_re[:q, :, None, :]
    lag = (jnp.einsum('kgop,gpi->kgoi', cp_re, bb_re, precision=HIGHEST)
           - jnp.einsum('kgop,gpi->kgoi', cp_im, bb_im, precision=HIGHEST))
    t_in = jnp.arange(q)[:, None]
    t_out = jnp.arange(q)[None, :]
    delta = t_out - t_in
    lag_t = jnp.transpose(lag, (1, 0, 3, 2))
    toep = lag_t[:, jnp.clip(delta, 0, q - 1)]
    toep = jnp.where((delta >= 0)[None, :, :, None, None], toep, 0.0)
    toep = jnp.transpose(toep, (0, 1, 3, 2, 4)).reshape(g, q * hh, q * hh)

    rv_re, rv_im = pw_re[:q][::-1], pw_im[:q][::-1]
    ws_re = rv_re[..., None] * bb_re[None] - rv_im[..., None] * bb_im[None]
    ws_im = rv_re[..., None] * bb_im[None] + rv_im[..., None] * bb_re[None]
    ws_re = jnp.transpose(ws_re, (1, 0, 3, 2)).reshape(g, q * hh, p)
    ws_im = jnp.transpose(ws_im, (1, 0, 3, 2)).reshape(g, q * hh, p)

    up_re, up_im = pw_re[1:], pw_im[1:]
    cy_re = c_re[None] * up_re[:, :, None, :] - c_im[None] * up_im[:, :, None, :]
    cy_im = c_re[None] * up_im[:, :, None, :] + c_im[None] * up_re[:, :, None, :]
    wy_re = jnp.transpose(cy_re, (1, 3, 0, 2)).reshape(g, p, q * hh)
    wy_im = -jnp.transpose(cy_im, (1, 3, 0, 2)).reshape(g, p, q * hh)

    def pair_rows(m):
        return m.reshape(g // 2, 2, *m.shape[1:])

    zeros_t = jnp.zeros((g // 2, q * hh, q * hh), F32)
    tp = pair_rows(toep)
    toep_p = jnp.concatenate([jnp.concatenate([tp[:, 0], zeros_t], axis=2),
                              jnp.concatenate([zeros_t, tp[:, 1]], axis=2)], axis=1)
    zs = jnp.zeros((g // 2, q * hh, p), F32)
    wsr, wsi = pair_rows(ws_re), pair_rows(ws_im)
    ws_p = jnp.concatenate([jnp.concatenate([wsr[:, 0], zs, wsi[:, 0], zs], axis=2),
                            jnp.concatenate([zs, wsr[:, 1], zs, wsi[:, 1]], axis=2)], axis=1)
    zy = jnp.zeros((g // 2, p, q * hh), F32)
    wyr, wyi = pair_rows(wy_re), pair_rows(wy_im)
    wy_p = jnp.concatenate([jnp.concatenate([wyr[:, 0], zy], axis=2),
                            jnp.concatenate([zy, wyr[:, 1]], axis=2),
                            jnp.concatenate([wyi[:, 0], zy], axis=2),
                            jnp.concatenate([zy, wyi[:, 1]], axis=2)], axis=1)
    aq_re = pw_re[q].reshape(g // 2, 2 * p)
    aq_im = pw_im[q].reshape(g // 2, 2 * p)
    a_p = jnp.concatenate([aq_re, aq_im], axis=1)
    return toep_p.astype(BF16), ws_p.astype(BF16), wy_p.astype(BF16), a_p


def _s5_layer(x, norm_g, shift, scale, gate, w_in, lam_re, lam_im, log_dt,
              b_re, b_im, c_re, c_im, d_skip, w_glu, b_glu):
    bsz, seq, d = x.shape
    groups = d // S5_GROUP
    pairs = groups // 2
    nc = seq // S5_Q
    tl = TOKEN_BLOCK
    tok = pl.BlockSpec((1, tl, d), lambda b, i: (b, i, 0))
    per_b = pl.BlockSpec((1, 1, d), lambda b, i: (b, 0, 0))

    u = pl.pallas_call(
        _s5_in_kernel,
        grid=(bsz, seq // tl),
        in_specs=[tok, _const_spec((1, d)), per_b, per_b, _const_spec((d, d))],
        out_specs=tok,
        out_shape=jax.ShapeDtypeStruct(x.shape, F32),
        compiler_params=_compiler_params(("parallel", "parallel"),
                                         2 * _nbytes((tl, d), F32) + _nbytes((d, d), BF16),
                                         3 * _nbytes((tl, d), F32)),
        name="s5_in",
    )(x, norm_g.reshape(1, d), shift, scale, w_in.astype(BF16))

    toep, ws, wy, a_pair = _s5_tables(lam_re, lam_im, log_dt, b_re, b_im, c_re, c_im)

    xp = u.reshape(bsz, nc, S5_Q, pairs, 2, S5_GROUP).transpose(0, 3, 1, 4, 2, 5)
    xp = xp.reshape(bsz, pairs, nc, 2 * S5_QW).astype(BF16)

    xspec = pl.BlockSpec((1, 1, nc, 2 * S5_QW), lambda b, p: (b, p, 0, 0))
    sspec = pl.BlockSpec((1, 1, nc, S5_PAIR_STATE), lambda b, p: (b, p, 0, 0))
    s_loc = pl.pallas_call(
        _s5_state_kernel,
        grid=(bsz, pairs),
        in_specs=[xspec, pl.BlockSpec((1, 2 * S5_QW, S5_PAIR_STATE), lambda b, p: (p, 0, 0))],
        out_specs=sspec,
        out_shape=jax.ShapeDtypeStruct((bsz, pairs, nc, S5_PAIR_STATE), F32),
        compiler_params=_compiler_params(
            ("parallel", "parallel"),
            _nbytes((nc, 2 * S5_QW), BF16) + _nbytes((nc, S5_PAIR_STATE), F32),
            _nbytes((nc, S5_PAIR_STATE), F32)),
        name="s5_chunk_state",
    )(xp, ws)

    rows = bsz * pairs
    s_t = s_loc.reshape(rows, nc, S5_PAIR_STATE).transpose(1, 0, 2)
    a_rows = jnp.tile(a_pair, (bsz, 1))
    cb = SCAN_BLOCK
    sblk = pl.BlockSpec((cb, rows, S5_PAIR_STATE), lambda i: (i, 0, 0))
    h_in = pl.pallas_call(
        _s5_scan_kernel,
        grid=(nc // cb,),
        in_specs=[_const_spec((rows, S5_PAIR_STATE)), sblk],
        out_specs=sblk,
        out_shape=jax.ShapeDtypeStruct((nc, rows, S5_PAIR_STATE), F32),
        scratch_shapes=[pltpu.VMEM((rows, S5_PAIR_STATE), F32)],
        compiler_params=_compiler_params(("arbitrary",),
                                         2 * _nbytes((cb, rows, S5_PAIR_STATE), F32),
                                         1024 * 1024),
        name="s5_chunk_scan",
    )(a_rows, s_t)
    h_in = h_in.transpose(1, 0, 2).reshape(bsz, pairs, nc, S5_PAIR_STATE)

    y_p = pl.pallas_call(
        _s5_y_kernel,
        grid=(bsz, pairs),
        in_specs=[xspec, sspec,
                  pl.BlockSpec((1, 2 * S5_QW, 2 * S5_QW), lambda b, p: (p, 0, 0)),
                  pl.BlockSpec((1, S5_PAIR_STATE, 2 * S5_QW), lambda b, p: (p, 0, 0))],
        out_specs=pl.BlockSpec((1, 1, nc, 2 * S5_QW), lambda b, p: (b, p, 0, 0)),
        out_shape=jax.ShapeDtypeStruct((bsz, pairs, nc, 2 * S5_QW), F32),
        compiler_params=_compiler_params(
            ("parallel", "parallel"),
            _nbytes((nc, 2 * S5_QW), BF16) + _nbytes((nc, S5_PAIR_STATE), F32)
            + _nbytes((nc, 2 * S5_QW), F32),
            2 * _nbytes((nc, 2 * S5_QW), F32)),
        name="s5_chunk_output",
    )(xp, h_in, toep, wy)

    y = y_p.reshape(bsz, pairs, nc, 2, S5_Q, S5_GROUP).transpose(0, 2, 4, 1, 3, 5)
    y = y.reshape(bsz, seq, d)

    return pl.pallas_call(
        _s5_out_kernel,
        grid=(bsz, seq // tl),
        in_specs=[tok, tok, tok, _const_spec((1, d)), _const_spec((d, 2 * d)),
                  _const_spec((1, 2 * d)), per_b],
        out_specs=tok,
        out_shape=jax.ShapeDtypeStruct(x.shape, F32),
        compiler_params=_compiler_params(("parallel", "parallel"),
                                         4 * _nbytes((tl, d), F32) + _nbytes((d, 2 * d), BF16),
                                         4 * _nbytes((tl, 2 * d), F32)),
        name="s5_out",
    )(y, u, x, d_skip.reshape(1, d), w_glu.astype(BF16), b_glu.reshape(1, 2 * d), gate)


def _ssd_in_kernel(x_ref, g_ref, sh_ref, sc_ref, wz_ref, wx_ref, wdt_ref, wdtt_ref,
                   z_ref, xbc_ref, dt_ref, dtt_ref):
    h = _rms_mod(x_ref[0], g_ref[...], sh_ref[0], sc_ref[0]).astype(BF16)
    z_ref[0] = jnp.dot(h, wz_ref[...], preferred_element_type=F32)
    xbc_ref[0] = jnp.dot(h, wx_ref[...], preferred_element_type=F32)
    dt_ref[0] = jnp.dot(h, wdt_ref[...], preferred_element_type=F32)
    dtt_ref[0] = lax.dot_general(wdtt_ref[...], h, (((1,), (1,)), ((), ())),
                                 preferred_element_type=F32)


def _ssd_core_kernel(xbc_ref, dt_ref, dtt_ref, cw_ref, cb_ref, bias_ref, biast_ref,
                     alog_ref, alogt_ref, dskip_ref, y_ref, ext_ref, st_ref,
                     *, d_inner, heads):
    q = M2_CHUNK
    n = M2_STATE
    hp = M2_HEAD_PAIR
    halo = V7X_SUBLANES

    @pl.when(pl.program_id(1) == 0)
    def _():
        ext_ref[0:halo, :] = jnp.zeros((halo, ext_ref.shape[1]), F32)
        st_ref[...] = jnp.zeros_like(st_ref)

    xr = xbc_ref[0]
    ext_ref[halo:halo + q, :] = xr
    acc = cb_ref[...] + cw_ref[M2_CONV - 1:M2_CONV, :] * xr
    for s in range(1, M2_CONV):
        acc += cw_ref[M2_CONV - 1 - s:M2_CONV - s, :] * ext_ref[pl.ds(halo - s, q), :]
    ext_ref[0:halo, :] = xr[q - halo:q, :]
    xbc = acc * jax.nn.sigmoid(acc)

    row = lax.broadcasted_iota(jnp.int32, (q, q), 0)
    col = lax.broadcasted_iota(jnp.int32, (q, q), 1)
    causal = row >= col
    dt = _softplus(dt_ref[0] + bias_ref[...])
    a_dt = dt * (-jnp.exp(alog_ref[...]))
    a_cs = jnp.dot(causal.astype(F32), a_dt, precision=HIGHEST, preferred_element_type=F32)
    dtt = _softplus(dtt_ref[0] + biast_ref[...])
    a_dtt = dtt * (-jnp.exp(alogt_ref[...]))
    a_cst = jnp.dot(a_dtt, (row <= col).astype(F32), precision=HIGHEST,
                    preferred_element_type=F32)
    state_w = jnp.exp(a_cst[:, q - 1:q] - a_cst) * dtt
    chunk_decay = jnp.exp(a_cs[q - 1:q, :])

    lane = lax.broadcasted_iota(jnp.int32, (q, hp), 1)
    first = lane < M2_HEADDIM
    first_row = lax.broadcasted_iota(jnp.int32, (1, hp), 1) < M2_HEADDIM
    heads_per_group = heads // M2_GROUPS
    pairs_per_group = heads_per_group // 2
    for g in range(M2_GROUPS):
        b_g = xbc[:, d_inner + g * n:d_inner + (g + 1) * n]
        c_g = xbc[:, d_inner + M2_GROUPS * n + g * n:d_inner + M2_GROUPS * n + (g + 1) * n]
        cb = lax.dot_general(c_g.astype(BF16), b_g.astype(BF16), (((1,), (1,)), ((), ())),
                             preferred_element_type=F32)
        b_t = b_g.T
        for j in range(pairs_per_group):
            h0 = g * heads_per_group + 2 * j
            sl = slice((h0 // 2) * hp, (h0 // 2 + 1) * hp)
            lhs, lhs_state = [], []
            for h in (h0, h0 + 1):
                bc = jnp.broadcast_to(a_cs[:, h:h + 1], (q, q))
                decay = jnp.exp(jnp.where(causal, bc - a_cst[h:h + 1, :], -jnp.inf))
                lhs.append(cb * decay * dtt[h:h + 1, :])
                lhs_state.append(b_t * state_w[h:h + 1, :])
            for h in (h0, h0 + 1):
                lhs.append(c_g * jnp.exp(jnp.broadcast_to(a_cs[:, h:h + 1], (q, n))))
            xs_p = xbc[:, sl]
            st_p = st_ref[:, sl]
            x0 = jnp.where(first, xs_p, 0.0)
            x1 = jnp.where(first, 0.0, xs_p)
            s0 = jnp.where(first, st_p, 0.0)
            s1 = jnp.where(first, 0.0, st_p)
            lhs = jnp.concatenate(lhs, axis=1).astype(BF16)
            rhs = jnp.concatenate([x0, x1, s0, s1], axis=0).astype(BF16)
            y = jnp.dot(lhs, rhs, preferred_element_type=F32)
            y_ref[0, :, sl] = y + xs_p * dskip_ref[:, sl]
            s_new = jnp.dot(jnp.concatenate(lhs_state, axis=1).astype(BF16),
                            jnp.concatenate([x0, x1], axis=0).astype(BF16),
                            preferred_element_type=F32)
            cd = jnp.where(first_row, chunk_decay[:, h0:h0 + 1], chunk_decay[:, h0 + 1:h0 + 2])
            st_ref[:, sl] = st_p * cd + s_new


def _ssd_out_kernel(y_ref, z_ref, x_ref, ng_ref, w_ref, gate_ref, o_ref):
    z = z_ref[0]
    y = y_ref[0] * (z * jax.nn.sigmoid(z))
    gw = y.shape[1] // M2_GROUPS
    parts = []
    for g in range(M2_GROUPS):
        yg = y[:, g * gw:(g + 1) * gw]
        parts.append(yg * lax.rsqrt(jnp.mean(yg * yg, axis=-1, keepdims=True) + NORM_EPS))
    y = jnp.concatenate(parts, axis=1) * ng_ref[...]
    out = jnp.dot(y.astype(BF16), w_ref[...], preferred_element_type=F32)
    o_ref[0] = x_ref[0] + gate_ref[0] * out


def _ssd_layer(x, norm_g, shift, scale, gate, w_in, conv_w, conv_b, dt_bias, a_log,
               d_skip, gn_g, w_out):
    bsz, seq, d = x.shape
    d_inner = w_out.shape[0]
    heads = d_inner // M2_HEADDIM
    conv_dim = d_inner + 2 * M2_GROUPS * M2_STATE
    lanes = V7X_LANES
    tl = TOKEN_BLOCK
    tok = pl.BlockSpec((1, tl, d), lambda b, i: (b, i, 0))
    per_b = pl.BlockSpec((1, 1, d), lambda b, i: (b, 0, 0))

    w_z = w_in[:, :d_inner].astype(BF16)
    w_x = w_in[:, d_inner:d_inner + conv_dim].astype(BF16)
    w_dt = w_in[:, d_inner + conv_dim:]
    w_dt_pad = jnp.zeros((d, lanes), F32).at[:, :heads].set(w_dt).astype(BF16)
    w_dt_t = w_dt.T.astype(BF16)

    def tok_n(width):
        return pl.BlockSpec((1, tl, width), lambda b, i: (b, i, 0))

    z, xbc, dt_raw, dt_raw_t = pl.pallas_call(
        _ssd_in_kernel,
        grid=(bsz, seq // tl),
        in_specs=[tok, _const_spec((1, d)), per_b, per_b, _const_spec((d, d_inner)),
                  _const_spec((d, conv_dim)), _const_spec((d, lanes)), _const_spec((heads, d))],
        out_specs=[tok_n(d_inner), tok_n(conv_dim), tok_n(lanes),
                   pl.BlockSpec((1, heads, tl), lambda b, i: (b, 0, i))],
        out_shape=[jax.ShapeDtypeStruct((bsz, seq, d_inner), F32),
                   jax.ShapeDtypeStruct((bsz, seq, conv_dim), F32),
                   jax.ShapeDtypeStruct((bsz, seq, lanes), F32),
                   jax.ShapeDtypeStruct((bsz, heads, seq), F32)],
        compiler_params=_compiler_params(
            ("parallel", "parallel"),
            _nbytes((tl, d + d_inner + conv_dim + lanes), F32)
            + _nbytes((d, d_inner + conv_dim), BF16),
            _nbytes((tl, d_inner + conv_dim), F32)),
        name="ssd_in",
    )(x, norm_g.reshape(1, d), shift, scale, w_z, w_x, w_dt_pad, w_dt_t)

    q = M2_CHUNK
    pad_h = jnp.zeros((1, lanes), F32)
    y = pl.pallas_call(
        functools.partial(_ssd_core_kernel, d_inner=d_inner, heads=heads),
        grid=(bsz, seq // q),
        in_specs=[pl.BlockSpec((1, q, conv_dim), lambda b, c: (b, c, 0)),
                  pl.BlockSpec((1, q, lanes), lambda b, c: (b, c, 0)),
                  pl.BlockSpec((1, heads, q), lambda b, c: (b, 0, c)),
                  _const_spec((M2_CONV, conv_dim)), _const_spec((1, conv_dim)),
                  _const_spec((1, lanes)), _const_spec((heads, 1)),
                  _const_spec((1, lanes)), _const_spec((heads, 1)),
                  _const_spec((1, d_inner))],
        out_specs=pl.BlockSpec((1, q, d_inner), lambda b, c: (b, c, 0)),
        out_shape=jax.ShapeDtypeStruct((bsz, seq, d_inner), F32),
        scratch_shapes=[pltpu.VMEM((V7X_SUBLANES + q, conv_dim), F32),
                        pltpu.VMEM((M2_STATE, d_inner), F32)],
        compiler_params=_compiler_params(
            ("arbitrary", "arbitrary"),
            _nbytes((q, conv_dim + d_inner + 2 * lanes), F32),
            8 * _nbytes((q, conv_dim), F32)),
        name="ssd_core",
    )(xbc, dt_raw, dt_raw_t, conv_w, conv_b.reshape(1, conv_dim),
      pad_h.at[0, :heads].set(dt_bias), dt_bias.reshape(heads, 1),
      pad_h.at[0, :heads].set(a_log), a_log.reshape(heads, 1),
      jnp.repeat(d_skip, M2_HEADDIM).reshape(1, d_inner))

    return pl.pallas_call(
        _ssd_out_kernel,
        grid=(bsz, seq // tl),
        in_specs=[tok_n(d_inner), tok_n(d_inner), tok, _const_spec((1, d_inner)),
                  _const_spec((d_inner, d)), per_b],
        out_specs=tok,
        out_shape=jax.ShapeDtypeStruct(x.shape, F32),
        compiler_params=_compiler_params(
            ("parallel", "parallel"),
            _nbytes((tl, 2 * d_inner + 2 * d), F32) + _nbytes((d_inner, d), BF16),
            3 * _nbytes((tl, d_inner), F32)),
        name="ssd_out",
    )(y, z, x, gn_g.reshape(1, d_inner), w_out.astype(BF16), gate)


def kernel(x, c, ada_w, ada_b, norm_mix_g, norm_mlp_g, mlp_w1, mlp_w2, s5_w_in, s5_lambda_re, s5_lambda_im, s5_log_dt, s5_b_re, s5_b_im, s5_c_re, s5_c_im, s5_d, s5_w_glu, s5_b_glu, m2_w_in, m2_conv_w, m2_conv_b, m2_dt_bias, m2_a_log, m2_d, m2_norm_g, m2_w_out, final_norm_g):
    depth = ada_w.shape[0]
    bsz, _, d = x.shape
    mod = _ada_modulation(c, ada_w, ada_b).reshape(depth, bsz, N_MOD, 1, d)
    for i in range(depth):
        sh1, sc1, g1, sh2, sc2, g2 = (mod[i, :, k] for k in range(N_MOD))
        j = i // 2
        if i % 2 == 0:
            x = _s5_layer(x, norm_mix_g[i], sh1, sc1, g1, s5_w_in[j], s5_lambda_re[j],
                          s5_lambda_im[j], s5_log_dt[j], s5_b_re[j], s5_b_im[j],
                          s5_c_re[j], s5_c_im[j], s5_d[j], s5_w_glu[j], s5_b_glu[j])
        else:
            x = _ssd_layer(x, norm_mix_g[i], sh1, sc1, g1, m2_w_in[j], m2_conv_w[j],
                           m2_conv_b[j], m2_dt_bias[j], m2_a_log[j], m2_d[j],
                           m2_norm_g[j], m2_w_out[j])
        x = _mlp_layer(x, norm_mlp_g[i], sh2, sc2, g2, mlp_w1[i], mlp_w2[i],
                       final_g=final_norm_g if i == depth - 1 else None)
    return x
```

```python
import functools

import jax
import jax.numpy as jnp
from jax import lax
from jax.experimental import pallas as pl
from jax.experimental.pallas import tpu as pltpu

F32 = jnp.float32
BF16 = jnp.bfloat16
HIGHEST = lax.Precision.HIGHEST
NORM_EPS = 1e-5
N_MOD = 6

V7X_LANES = 128
V7X_SUBLANES = 8
V7X_SCOPED_VMEM_BYTES = 60000 * 1024

S5_GROUP = 16
S5_STATE = 64
S5_Q = 16
S5_QW = S5_Q * S5_GROUP
S5_PAIR_STATE = 4 * S5_STATE
S5_CHUNK_ROWS = 256
S5_IN_OFFSETS = 4
S5_OUT_OFFSETS = 2

M2_HEADDIM = 64
M2_GROUPS = 4
M2_STATE = 128
M2_CONV = 4
M2_CHUNK = 128
M2_HEAD_PAIR = 2 * M2_HEADDIM

TOKEN_BLOCK = 512
SCAN_BLOCK = 64


def _compiler_params(semantics, block_bytes, temp_bytes):
    want = 2 * block_bytes + temp_bytes
    return pltpu.CompilerParams(
        dimension_semantics=semantics,
        vmem_limit_bytes=int(min(V7X_SCOPED_VMEM_BYTES, max(want, 16 * 1024 * 1024))))


def _nbytes(shape, dtype):
    n = 1
    for s in shape:
        n *= s
    return n * jnp.dtype(dtype).itemsize


def _const_spec(shape):
    return pl.BlockSpec(shape, lambda *_: (0,) * len(shape), pipeline_mode=pl.Buffered(1))


def _rms_mod(x, g, shift, scale):
    y = x * lax.rsqrt(jnp.mean(x * x, axis=-1, keepdims=True) + NORM_EPS)
    return (y * g) * (1.0 + scale) + shift


def _softplus(x):
    return jnp.maximum(x, 0.0) + jnp.log1p(jnp.exp(-jnp.abs(x)))


def _ada_kernel(c_ref, w_ref, b_ref, o_ref):
    cond = jax.nn.silu(c_ref[...])
    o_ref[0] = jnp.dot(cond, w_ref[0], precision=HIGHEST,
                       preferred_element_type=F32) + b_ref[0]


def _ada_modulation(c, ada_w, ada_b):
    depth, d, n = ada_w.shape
    bsz = c.shape[0]
    rows = V7X_SUBLANES
    nb = n // 4
    c_pad = jnp.zeros((rows, d), F32).at[:bsz].set(c)
    out = pl.pallas_call(
        _ada_kernel,
        grid=(depth, n // nb),
        in_specs=[pl.BlockSpec((rows, d), lambda i, j: (0, 0)),
                  pl.BlockSpec((1, d, nb), lambda i, j: (i, 0, j)),
                  pl.BlockSpec((1, 1, nb), lambda i, j: (i, 0, j))],
        out_specs=pl.BlockSpec((1, rows, nb), lambda i, j: (i, 0, j)),
        out_shape=jax.ShapeDtypeStruct((depth, rows, n), F32),
        compiler_params=_compiler_params(("arbitrary", "arbitrary"),
                                         _nbytes((d, nb), F32), 4 * 1024 * 1024),
        name="ada_modulation",
    )(c_pad, ada_w, ada_b.reshape(depth, 1, n))
    return out[:, :bsz, :]


def _mlp_kernel(x_ref, g_ref, sh_ref, sc_ref, gate_ref, w1_ref, w2_ref, *rest, final):
    if final:
        fg_ref, o_ref = rest
    else:
        (o_ref,) = rest
    x = x_ref[0]
    h = _rms_mod(x, g_ref[...], sh_ref[0], sc_ref[0])
    a = jnp.maximum(jnp.dot(h.astype(BF16), w1_ref[...], preferred_element_type=F32), 0.0)
    y = jnp.dot((a * a).astype(BF16), w2_ref[...], preferred_element_type=F32)
    out = x + gate_ref[0] * y
    if final:
        out = out * lax.rsqrt(jnp.mean(out * out, axis=-1, keepdims=True) + NORM_EPS)
        out = out * fg_ref[...]
    o_ref[0] = out


def _mlp_layer(x, norm_g, shift, scale, gate, w1, w2, final_g=None):
    bsz, seq, d = x.shape
    f = w1.shape[1]
    tl = TOKEN_BLOCK
    tok = pl.BlockSpec((1, tl, d), lambda b, i: (b, i, 0))
    per_b = pl.BlockSpec((1, 1, d), lambda b, i: (b, 0, 0))
    in_specs = [tok, _const_spec((1, d)), per_b, per_b, per_b,
                _const_spec((d, f)), _const_spec((f, d))]
    args = [x, norm_g.reshape(1, d), shift, scale, gate, w1.astype(BF16), w2.astype(BF16)]
    if final_g is not None:
        in_specs.append(_const_spec((1, d)))
        args.append(final_g.reshape(1, d))
    return pl.pallas_call(
        functools.partial(_mlp_kernel, final=final_g is not None),
        grid=(bsz, seq // tl),
        in_specs=in_specs,
        out_specs=tok,
        out_shape=jax.ShapeDtypeStruct(x.shape, F32),
        compiler_params=_compiler_params(
            ("parallel", "parallel"),
            2 * _nbytes((tl, d), F32) + _nbytes((d, f), BF16),
            _nbytes((tl, f), F32) * 2 + _nbytes((tl, d), F32) * 2),
        name="mlp",
    )(*args)


def _s5_in_kernel(x_ref, g_ref, sh_ref, sc_ref, wt_ref, u_ref):
    d = wt_ref.shape[0]
    for j in range(u_ref.shape[1]):
        h = _rms_mod(x_ref[0, :, j * d:(j + 1) * d], g_ref[...], sh_ref[0], sc_ref[0])
        u_ref[0, j] = lax.dot_general(wt_ref[...], h.astype(BF16), (((1,), (1,)), ((), ())),
                                      preferred_element_type=F32)


def _s5_state_kernel(x_ref, ws_ref, s_ref):
    q, rows, nc = x_ref.shape[1:]
    xt = x_ref[0].reshape(q * rows, nc).astype(BF16)
    s_ref[0, 0] = jnp.dot(ws_ref[0], xt, preferred_element_type=F32).T


def _s5_scan_kernel(a_ref, s_ref, o_ref, h_ref):
    half = a_ref.shape[1] // 2

    @pl.when(pl.program_id(0) == 0)
    def _():
        h_ref[...] = jnp.zeros_like(h_ref)

    a_re = a_ref[:, :half]
    a_im = a_ref[:, half:]

    def body(c, carry):
        h_re, h_im = carry
        o_ref[c, :, :half] = h_re
        o_ref[c, :, half:] = h_im
        s = s_ref[c]
        return (a_re * h_re - a_im * h_im + s[:, :half],
                a_re * h_im + a_im * h_re + s[:, half:])

    h_re, h_im = lax.fori_loop(0, s_ref.shape[0], body,
                               (h_ref[:, :half], h_ref[:, half:]))
    h_ref[:, :half] = h_re
    h_ref[:, half:] = h_im


def _s5_y_kernel(x_ref, h_ref, t_ref, wy_ref, y_ref):
    q, rows, nc = x_ref.shape[1:]
    xt = x_ref[0].reshape(q * rows, nc).astype(BF16)
    y = jnp.dot(t_ref[0], xt, preferred_element_type=F32)
    y += jnp.dot(wy_ref[0], h_ref[0, 0].T.astype(BF16), preferred_element_type=F32)
    y_ref[0] = y.reshape(q, rows, nc)


def _s5_out_kernel(y_ref, u_ref, x_ref, d_ref, w_ref, b_ref, gate_ref, o_ref):
    d = w_ref.shape[0]
    tg = y_ref.shape[1]
    g = [jax.nn.gelu(y_ref[0, j] + d_ref[...] * u_ref[0, j]).T for j in range(tg)]
    g = jnp.concatenate(g, axis=0).astype(BF16)
    ab = jnp.dot(g, w_ref[...], preferred_element_type=F32) + b_ref[...]
    mix = ab[:, :d] * jax.nn.sigmoid(ab[:, d:])
    nc = mix.shape[0] // tg
    for j in range(tg):
        o_ref[0, :, j * d:(j + 1) * d] = (x_ref[0, :, j * d:(j + 1) * d]
                                          + gate_ref[0] * mix[j * nc:(j + 1) * nc])


def _s5_tables(lam_re, lam_im, log_dt, b_re, b_im, c_re, c_im):
    g, p = lam_re.shape
    q, hh = S5_Q, S5_GROUP
    dt = jnp.exp(log_dt)[:, None]
    ld_re, ld_im = lam_re * dt, lam_im * dt

    def lam_pow(k):
        mag = jnp.exp(ld_re[None] * k[:, None, None])
        ang = ld_im[None] * k[:, None, None]
        return mag * jnp.cos(ang), mag * jnp.sin(ang)

    lb_re, lb_im = lam_pow(jnp.ones((1,), F32))
    lb_re, lb_im = lb_re[0], lb_im[0]
    den = lam_re * lam_re + lam_im * lam_im
    f_re = ((lb_re - 1.0) * lam_re + lb_im * lam_im) / den
    f_im = (lb_im * lam_re - (lb_re - 1.0) * lam_im) / den
    bb_re = f_re[..., None] * b_re - f_im[..., None] * b_im
    bb_im = f_re[..., None] * b_im + f_im[..., None] * b_re

    steps = jnp.arange(q + 1, dtype=F32)
    pw_re, pw_im = lam_pow(steps)

    cp_re = c_re[None] * pw_re[:q, :, None, :] - c_im[None] * pw_im[:q, :, None, :]
    cp_im = c_re[None] * pw_im[:q, :, None, :] + c_im[None] * pw_re[:q, :, None, :]
    lag = (jnp.einsum('kgop,gpi->kgoi', cp_re, bb_re, precision=HIGHEST)
           - jnp.einsum('kgop,gpi->kgoi', cp_im, bb_im, precision=HIGHEST))
    pairs = g // 2
    same = jnp.eye(2, dtype=F32)

    delta = jnp.arange(q)[:, None] - jnp.arange(q)[None, :]
    lag_m = jnp.where((delta >= 0)[:, :, None, None, None], lag[jnp.clip(delta, 0, q - 1)], 0.0)
    lag_m = lag_m.reshape(q, q, pairs, 2, hh, hh)
    toep = lag_m[:, :, :, :, None] * same[None, None, None, :, :, None, None]
    toep = jnp.transpose(toep, (2, 0, 3, 5, 1, 4, 6)).reshape(pairs, 2 * q * hh, 2 * q * hh)

    rv_re, rv_im = pw_re[:q][::-1], pw_im[:q][::-1]
    ws_re = rv_re[..., None] * bb_re[None] - rv_im[..., None] * bb_im[None]
    ws_im = rv_re[..., None] * bb_im[None] + rv_im[..., None] * bb_re[None]
    ws = jnp.stack([ws_re, ws_im]).reshape(2, q, pairs, 2, p, hh)
    ws = ws[:, :, :, :, None] * same[None, None, None, :, :, None, None]
    ws = jnp.transpose(ws, (2, 0, 4, 5, 1, 3, 6)).reshape(pairs, 4 * p, 2 * q * hh)

    up_re, up_im = pw_re[1:], pw_im[1:]
    cy_re = c_re[None] * up_re[:, :, None, :] - c_im[None] * up_im[:, :, None, :]
    cy_im = c_re[None] * up_im[:, :, None, :] + c_im[None] * up_re[:, :, None, :]
    wy = jnp.stack([cy_re, -cy_im]).reshape(2, q, pairs, 2, hh, p)
    wy = wy[:, :, :, :, None] * same[None, None, None, :, :, None, None]
    wy = jnp.transpose(wy, (2, 1, 3, 5, 0, 4, 6)).reshape(pairs, 2 * q * hh, 4 * p)

    a_p = jnp.concatenate([pw_re[q].reshape(pairs, 2 * p), pw_im[q].reshape(pairs, 2 * p)], axis=1)
    return toep.astype(BF16), ws.astype(BF16), wy.astype(BF16), a_p


def _s5_layer(x, norm_g, shift, scale, gate, w_in, lam_re, lam_im, log_dt,
              b_re, b_im, c_re, c_im, d_skip, w_glu, b_glu):
    bsz, seq, d = x.shape
    groups = d // S5_GROUP
    pairs = groups // 2
    nc = seq // S5_Q
    pair_rows = 2 * S5_GROUP
    per_b = pl.BlockSpec((1, 1, d), lambda b, i, k: (b, 0, 0))

    xv = x.reshape(bsz, nc, S5_Q * d)
    ncb = S5_CHUNK_ROWS

    def tok_spec(tg):
        return pl.BlockSpec((1, ncb, tg * d), lambda b, i, k: (b, i, k))

    def chan_spec(tg):
        return pl.BlockSpec((1, tg, d, ncb), lambda b, i, k: (b, k, 0, i))

    tg = S5_IN_OFFSETS
    ut = pl.pallas_call(
        _s5_in_kernel,
        grid=(bsz, nc // ncb, S5_Q // tg),
        in_specs=[tok_spec(tg), _const_spec((1, d)), per_b, per_b, _const_spec((d, d))],
        out_specs=chan_spec(tg),
        out_shape=jax.ShapeDtypeStruct((bsz, S5_Q, d, nc), F32),
        compiler_params=_compiler_params(("parallel", "parallel", "parallel"),
                                         2 * _nbytes((ncb, tg * d), F32) + _nbytes((d, d), BF16),
                                         4 * _nbytes((ncb, d), F32)),
        name="s5_in",
    )(xv, norm_g.reshape(1, d), shift, scale, w_in.T.astype(BF16))

    toep, ws, wy, a_pair = _s5_tables(lam_re, lam_im, log_dt, b_re, b_im, c_re, c_im)

    xspec = pl.BlockSpec((1, S5_Q, pair_rows, nc), lambda b, p: (b, 0, p, 0))
    sspec = pl.BlockSpec((1, 1, nc, S5_PAIR_STATE), lambda b, p: (b, p, 0, 0))
    s_loc = pl.pallas_call(
        _s5_state_kernel,
        grid=(bsz, pairs),
        in_specs=[xspec, pl.BlockSpec((1, S5_PAIR_STATE, 2 * S5_QW), lambda b, p: (p, 0, 0))],
        out_specs=sspec,
        out_shape=jax.ShapeDtypeStruct((bsz, pairs, nc, S5_PAIR_STATE), F32),
        compiler_params=_compiler_params(
            ("parallel", "parallel"),
            _nbytes((nc, 2 * S5_QW), F32) + _nbytes((nc, S5_PAIR_STATE), F32),
            _nbytes((nc, 2 * S5_QW), F32) + 2 * _nbytes((nc, S5_PAIR_STATE), F32)),
        name="s5_chunk_state",
    )(ut, ws)

    rows = bsz * pairs
    s_t = s_loc.reshape(rows, nc, S5_PAIR_STATE).transpose(1, 0, 2)
    a_rows = jnp.tile(a_pair, (bsz, 1))
    cb = SCAN_BLOCK
    sblk = pl.BlockSpec((cb, rows, S5_PAIR_STATE), lambda i: (i, 0, 0))
    h_in = pl.pallas_call(
        _s5_scan_kernel,
        grid=(nc // cb,),
        in_specs=[_const_spec((rows, S5_PAIR_STATE)), sblk],
        out_specs=sblk,
        out_shape=jax.ShapeDtypeStruct((nc, rows, S5_PAIR_STATE), F32),
        scratch_shapes=[pltpu.VMEM((rows, S5_PAIR_STATE), F32)],
        compiler_params=_compiler_params(("arbitrary",),
                                         2 * _nbytes((cb, rows, S5_PAIR_STATE), F32),
                                         1024 * 1024),
        name="s5_chunk_scan",
    )(a_rows, s_t)
    h_in = h_in.transpose(1, 0, 2).reshape(bsz, pairs, nc, S5_PAIR_STATE)

    yt = pl.pallas_call(
        _s5_y_kernel,
        grid=(bsz, pairs),
        in_specs=[xspec, sspec,
                  pl.BlockSpec((1, 2 * S5_QW, 2 * S5_QW), lambda b, p: (p, 0, 0)),
                  pl.BlockSpec((1, 2 * S5_QW, S5_PAIR_STATE), lambda b, p: (p, 0, 0))],
        out_specs=xspec,
        out_shape=jax.ShapeDtypeStruct((bsz, S5_Q, d, nc), F32),
        compiler_params=_compiler_params(
            ("parallel", "parallel"),
            2 * _nbytes((nc, 2 * S5_QW), F32) + _nbytes((nc, S5_PAIR_STATE), F32),
            3 * _nbytes((nc, 2 * S5_QW), F32)),
        name="s5_chunk_output",
    )(ut, h_in, toep, wy)

    tg = S5_OUT_OFFSETS
    out = pl.pallas_call(
        _s5_out_kernel,
        grid=(bsz, nc // ncb, S5_Q // tg),
        in_specs=[chan_spec(tg), chan_spec(tg), tok_spec(tg), _const_spec((d, 1)),
                  _const_spec((d, 2 * d)), _const_spec((1, 2 * d)), per_b],
        out_specs=tok_spec(tg),
        out_shape=jax.ShapeDtypeStruct(xv.shape, F32),
        compiler_params=_compiler_params(("parallel", "parallel", "parallel"),
                                         4 * _nbytes((ncb, tg * d), F32) + _nbytes((d, 2 * d), BF16),
                                         6 * _nbytes((tg * ncb, 2 * d), F32)),
        name="s5_out",
    )(yt, ut, xv, d_skip.reshape(d, 1), w_glu.astype(BF16), b_glu.reshape(1, 2 * d), gate)
    return out.reshape(bsz, seq, d)


def _ssd_in_kernel(x_ref, g_ref, sh_ref, sc_ref, wz_ref, wx_ref, wdt_ref, wdtt_ref,
                   z_ref, xbc_ref, dt_ref, dtt_ref):
    h = _rms_mod(x_ref[0], g_ref[...], sh_ref[0], sc_ref[0]).astype(BF16)
    z_ref[0] = jnp.dot(h, wz_ref[...], preferred_element_type=F32)
    xbc_ref[0] = jnp.dot(h, wx_ref[...], preferred_element_type=F32)
    dt_ref[0] = jnp.dot(h, wdt_ref[...], preferred_element_type=F32)
    dtt_ref[0] = lax.dot_general(wdtt_ref[...], h, (((1,), (1,)), ((), ())),
                                 preferred_element_type=F32)


def _ssd_core_kernel(xbc_ref, dt_ref, dtt_ref, cw_ref, cb_ref, bias_ref, biast_ref,
                     alog_ref, alogt_ref, dskip_ref, y_ref, ext_ref, st_ref,
                     *, d_inner, heads):
    q = M2_CHUNK
    n = M2_STATE
    hp = M2_HEAD_PAIR
    halo = V7X_SUBLANES

    @pl.when(pl.program_id(1) == 0)
    def _():
        ext_ref[0:halo, :] = jnp.zeros((halo, ext_ref.shape[1]), F32)
        st_ref[...] = jnp.zeros_like(st_ref)

    xr = xbc_ref[0]
    ext_ref[halo:halo + q, :] = xr
    acc = cb_ref[...] + cw_ref[M2_CONV - 1:M2_CONV, :] * xr
    for s in range(1, M2_CONV):
        acc += cw_ref[M2_CONV - 1 - s:M2_CONV - s, :] * ext_ref[pl.ds(halo - s, q), :]
    ext_ref[0:halo, :] = xr[q - halo:q, :]
    xbc = acc * jax.nn.sigmoid(acc)

    row = lax.broadcasted_iota(jnp.int32, (q, q), 0)
    col = lax.broadcasted_iota(jnp.int32, (q, q), 1)
    causal = row >= col
    dt = _softplus(dt_ref[0] + bias_ref[...])
    a_dt = dt * (-jnp.exp(alog_ref[...]))
    a_cs = jnp.dot(causal.astype(F32), a_dt, precision=HIGHEST, preferred_element_type=F32)
    dtt = _softplus(dtt_ref[0] + biast_ref[...])
    a_dtt = dtt * (-jnp.exp(alogt_ref[...]))
    a_cst = jnp.dot(a_dtt, (row <= col).astype(F32), precision=HIGHEST,
                    preferred_element_type=F32)
    state_w = jnp.exp(a_cst[:, q - 1:q] - a_cst) * dtt
    chunk_decay = jnp.exp(a_cs[q - 1:q, :])

    lane = lax.broadcasted_iota(jnp.int32, (q, hp), 1)
    first = lane < M2_HEADDIM
    first_row = lax.broadcasted_iota(jnp.int32, (1, hp), 1) < M2_HEADDIM
    heads_per_group = heads // M2_GROUPS
    pairs_per_group = heads_per_group // 2
    for g in range(M2_GROUPS):
        b_g = xbc[:, d_inner + g * n:d_inner + (g + 1) * n]
        c_g = xbc[:, d_inner + M2_GROUPS * n + g * n:d_inner + M2_GROUPS * n + (g + 1) * n]
        cb = lax.dot_general(c_g.astype(BF16), b_g.astype(BF16), (((1,), (1,)), ((), ())),
                             preferred_element_type=F32)
        b_t = b_g.T
        for j in range(pairs_per_group):
            h0 = g * heads_per_group + 2 * j
            sl = slice((h0 // 2) * hp, (h0 // 2 + 1) * hp)
            lhs, lhs_state = [], []
            for h in (h0, h0 + 1):
                bc = jnp.broadcast_to(a_cs[:, h:h + 1], (q, q))
                decay = jnp.exp(jnp.where(causal, bc - a_cst[h:h + 1, :], -jnp.inf))
                lhs.append(cb * decay * dtt[h:h + 1, :])
                lhs_state.append(b_t * state_w[h:h + 1, :])
            for h in (h0, h0 + 1):
                lhs.append(c_g * jnp.exp(jnp.broadcast_to(a_cs[:, h:h + 1], (q, n))))
            xs_p = xbc[:, sl]
            st_p = st_ref[:, sl]
            x0 = jnp.where(first, xs_p, 0.0)
            x1 = jnp.where(first, 0.0, xs_p)
            s0 = jnp.where(first, st_p, 0.0)
            s1 = jnp.where(first, 0.0, st_p)
            lhs = jnp.concatenate(lhs, axis=1).astype(BF16)
            rhs = jnp.concatenate([x0, x1, s0, s1], axis=0).astype(BF16)
            y = jnp.dot(lhs, rhs, preferred_element_type=F32)
            y_ref[0, :, sl] = y + xs_p * dskip_ref[:, sl]
            s_new = jnp.dot(jnp.concatenate(lhs_state, axis=1).astype(BF16),
                            jnp.concatenate([x0, x1], axis=0).astype(BF16),
                            preferred_element_type=F32)
            cd = jnp.where(first_row, chunk_decay[:, h0:h0 + 1], chunk_decay[:, h0 + 1:h0 + 2])
            st_ref[:, sl] = st_p * cd + s_new


def _ssd_out_kernel(y_ref, z_ref, x_ref, ng_ref, w_ref, gate_ref, o_ref):
    z = z_ref[0]
    y = y_ref[0] * (z * jax.nn.sigmoid(z))
    gw = y.shape[1] // M2_GROUPS
    parts = []
    for g in range(M2_GROUPS):
        yg = y[:, g * gw:(g + 1) * gw]
        parts.append(yg * lax.rsqrt(jnp.mean(yg * yg, axis=-1, keepdims=True) + NORM_EPS))
    y = jnp.concatenate(parts, axis=1) * ng_ref[...]
    out = jnp.dot(y.astype(BF16), w_ref[...], preferred_element_type=F32)
    o_ref[0] = x_ref[0] + gate_ref[0] * out


def _ssd_layer(x, norm_g, shift, scale, gate, w_in, conv_w, conv_b, dt_bias, a_log,
               d_skip, gn_g, w_out):
    bsz, seq, d = x.shape
    d_inner = w_out.shape[0]
    heads = d_inner // M2_HEADDIM
    conv_dim = d_inner + 2 * M2_GROUPS * M2_STATE
    lanes = V7X_LANES
    tl = TOKEN_BLOCK
    tok = pl.BlockSpec((1, tl, d), lambda b, i: (b, i, 0))
    per_b = pl.BlockSpec((1, 1, d), lambda b, i: (b, 0, 0))

    w_z = w_in[:, :d_inner].astype(BF16)
    w_x = w_in[:, d_inner:d_inner + conv_dim].astype(BF16)
    w_dt = w_in[:, d_inner + conv_dim:]
    w_dt_pad = jnp.zeros((d, lanes), F32).at[:, :heads].set(w_dt).astype(BF16)
    w_dt_t = w_dt.T.astype(BF16)

    def tok_n(width):
        return pl.BlockSpec((1, tl, width), lambda b, i: (b, i, 0))

    z, xbc, dt_raw, dt_raw_t = pl.pallas_call(
        _ssd_in_kernel,
        grid=(bsz, seq // tl),
        in_specs=[tok, _const_spec((1, d)), per_b, per_b, _const_spec((d, d_inner)),
                  _const_spec((d, conv_dim)), _const_spec((d, lanes)), _const_spec((heads, d))],
        out_specs=[tok_n(d_inner), tok_n(conv_dim), tok_n(lanes),
                   pl.BlockSpec((1, heads, tl), lambda b, i: (b, 0, i))],
        out_shape=[jax.ShapeDtypeStruct((bsz, seq, d_inner), F32),
                   jax.ShapeDtypeStruct((bsz, seq, conv_dim), F32),
                   jax.ShapeDtypeStruct((bsz, seq, lanes), F32),
                   jax.ShapeDtypeStruct((bsz, heads, seq), F32)],
        compiler_params=_compiler_params(
            ("parallel", "parallel"),
            _nbytes((tl, d + d_inner + conv_dim + lanes), F32)
            + _nbytes((d, d_inner + conv_dim), BF16),
            _nbytes((tl, d_inner + conv_dim), F32)),
        name="ssd_in",
    )(x, norm_g.reshape(1, d), shift, scale, w_z, w_x, w_dt_pad, w_dt_t)

    q = M2_CHUNK
    pad_h = jnp.zeros((1, lanes), F32)
    y = pl.pallas_call(
        functools.partial(_ssd_core_kernel, d_inner=d_inner, heads=heads),
        grid=(bsz, seq // q),
        in_specs=[pl.BlockSpec((1, q, conv_dim), lambda b, c: (b, c, 0)),
                  pl.BlockSpec((1, q, lanes), lambda b, c: (b, c, 0)),
                  pl.BlockSpec((1, heads, q), lambda b, c: (b, 0, c)),
                  _const_spec((M2_CONV, conv_dim)), _const_spec((1, conv_dim)),
                  _const_spec((1, lanes)), _const_spec((heads, 1)),
                  _const_spec((1, lanes)), _const_spec((heads, 1)),
                  _const_spec((1, d_inner))],
        out_specs=pl.BlockSpec((1, q, d_inner), lambda b, c: (b, c, 0)),
        out_shape=jax.ShapeDtypeStruct((bsz, seq, d_inner), F32),
        scratch_shapes=[pltpu.VMEM((V7X_SUBLANES + q, conv_dim), F32),
                        pltpu.VMEM((M2_STATE, d_inner), F32)],
        compiler_params=_compiler_params(
            ("arbitrary", "arbitrary"),
            _nbytes((q, conv_dim + d_inner + 2 * lanes), F32),
            8 * _nbytes((q, conv_dim), F32)),
        name="ssd_core",
    )(xbc, dt_raw, dt_raw_t, conv_w, conv_b.reshape(1, conv_dim),
      pad_h.at[0, :heads].set(dt_bias), dt_bias.reshape(heads, 1),
      pad_h.at[0, :heads].set(a_log), a_log.reshape(heads, 1),
      jnp.repeat(d_skip, M2_HEADDIM).reshape(1, d_inner))

    return pl.pallas_call(
        _ssd_out_kernel,
        grid=(bsz, seq // tl),
        in_specs=[tok_n(d_inner), tok_n(d_inner), tok, _const_spec((1, d_inner)),
                  _const_spec((d_inner, d)), per_b],
        out_specs=tok,
        out_shape=jax.ShapeDtypeStruct(x.shape, F32),
        compiler_params=_compiler_params(
            ("parallel", "parallel"),
            _nbytes((tl, 2 * d_inner + 2 * d), F32) + _nbytes((d_inner, d), BF16),
            3 * _nbytes((tl, d_inner), F32)),
        name="ssd_out",
    )(y, z, x, gn_g.reshape(1, d_inner), w_out.astype(BF16), gate)


def kernel(x, c, ada_w, ada_b, norm_mix_g, norm_mlp_g, mlp_w1, mlp_w2, s5_w_in, s5_lambda_re, s5_lambda_im, s5_log_dt, s5_b_re, s5_b_im, s5_c_re, s5_c_im, s5_d, s5_w_glu, s5_b_glu, m2_w_in, m2_conv_w, m2_conv_b, m2_dt_bias, m2_a_log, m2_d, m2_norm_g, m2_w_out, final_norm_g):
    depth = ada_w.shape[0]
    bsz, _, d = x.shape
    mod = _ada_modulation(c, ada_w, ada_b).reshape(depth, bsz, N_MOD, 1, d)
    for i in range(depth):
        sh1, sc1, g1, sh2, sc2, g2 = (mod[i, :, k] for k in range(N_MOD))
        j = i // 2
        if i % 2 == 0:
            x = _s5_layer(x, norm_mix_g[i], sh1, sc1, g1, s5_w_in[j], s5_lambda_re[j],
                          s5_lambda_im[j], s5_log_dt[j], s5_b_re[j], s5_b_im[j],
                          s5_c_re[j], s5_c_im[j], s5_d[j], s5_w_glu[j], s5_b_glu[j])
        else:
            x = _ssd_layer(x, norm_mix_g[i], sh1, sc1, g1, m2_w_in[j], m2_conv_w[j],
                           m2_conv_b[j], m2_dt_bias[j], m2_a_log[j], m2_d[j],
                           m2_norm_g[j], m2_w_out[j])
        x = _mlp_layer(x, norm_mlp_g[i], sh2, sc2, g2, mlp_w1[i], mlp_w2[i],
                       final_g=final_norm_g if i == depth - 1 else None)
    return x
```

```python
import functools

import jax
import jax.numpy as jnp
from jax import lax
from jax.experimental import pallas as pl
from jax.experimental.pallas import tpu as pltpu

F32 = jnp.float32
BF16 = jnp.bfloat16
HIGHEST = lax.Precision.HIGHEST
NORM_EPS = 1e-5
N_MOD = 6

V7X_LANES = 128
V7X_SUBLANES = 8
V7X_SCOPED_VMEM_BYTES = 60000 * 1024

S5_GROUP = 16
S5_STATE = 64
S5_Q = 16
S5_QW = S5_Q * S5_GROUP
S5_PAIR_STATE = 4 * S5_STATE
S5_CHUNK_ROWS = 256
S5_IN_OFFSETS = 4
S5_OUT_OFFSETS = 2

M2_HEADDIM = 64
M2_GROUPS = 4
M2_STATE = 128
M2_CONV = 4
M2_CHUNK = 128
M2_HEAD_PAIR = 2 * M2_HEADDIM
M2_PHASES = V7X_SUBLANES
M2_PHASE_ROWS = M2_CHUNK // M2_PHASES
M2_BLOCK_CHUNKS = 4

TOKEN_BLOCK = 512
SCAN_BLOCK = 64


def _compiler_params(semantics, block_bytes, temp_bytes):
    want = 2 * block_bytes + temp_bytes
    return pltpu.CompilerParams(
        dimension_semantics=semantics,
        vmem_limit_bytes=int(min(V7X_SCOPED_VMEM_BYTES, max(want, 16 * 1024 * 1024))))


def _nbytes(shape, dtype):
    n = 1
    for s in shape:
        n *= s
    return n * jnp.dtype(dtype).itemsize


def _const_spec(shape):
    return pl.BlockSpec(shape, lambda *_: (0,) * len(shape), pipeline_mode=pl.Buffered(1))


def _rms_mod(x, g, shift, scale):
    y = x * lax.rsqrt(jnp.mean(x * x, axis=-1, keepdims=True) + NORM_EPS)
    return (y * g) * (1.0 + scale) + shift


def _softplus(x):
    return jnp.maximum(x, 0.0) + jnp.log1p(jnp.exp(-jnp.abs(x)))


def _ada_kernel(c_ref, w_ref, b_ref, o_ref):
    cond = jax.nn.silu(c_ref[...])
    o_ref[0] = jnp.dot(cond, w_ref[0], precision=HIGHEST,
                       preferred_element_type=F32) + b_ref[0]


def _ada_modulation(c, ada_w, ada_b):
    depth, d, n = ada_w.shape
    bsz = c.shape[0]
    rows = V7X_SUBLANES
    nb = n // 4
    c_pad = jnp.zeros((rows, d), F32).at[:bsz].set(c)
    out = pl.pallas_call(
        _ada_kernel,
        grid=(depth, n // nb),
        in_specs=[pl.BlockSpec((rows, d), lambda i, j: (0, 0)),
                  pl.BlockSpec((1, d, nb), lambda i, j: (i, 0, j)),
                  pl.BlockSpec((1, 1, nb), lambda i, j: (i, 0, j))],
        out_specs=pl.BlockSpec((1, rows, nb), lambda i, j: (i, 0, j)),
        out_shape=jax.ShapeDtypeStruct((depth, rows, n), F32),
        compiler_params=_compiler_params(("arbitrary", "arbitrary"),
                                         _nbytes((d, nb), F32), 4 * 1024 * 1024),
        name="ada_modulation",
    )(c_pad, ada_w, ada_b.reshape(depth, 1, n))
    return out[:, :bsz, :]


def _mlp_kernel(x_ref, g_ref, sh_ref, sc_ref, gate_ref, w1_ref, w2_ref, *rest, final):
    if final:
        fg_ref, o_ref = rest
    else:
        (o_ref,) = rest
    x = x_ref[0]
    h = _rms_mod(x, g_ref[...], sh_ref[0], sc_ref[0])
    a = jnp.maximum(jnp.dot(h.astype(BF16), w1_ref[...], preferred_element_type=F32), 0.0)
    y = jnp.dot((a * a).astype(BF16), w2_ref[...], preferred_element_type=F32)
    out = x + gate_ref[0] * y
    if final:
        out = out * lax.rsqrt(jnp.mean(out * out, axis=-1, keepdims=True) + NORM_EPS)
        out = out * fg_ref[...]
    o_ref[0] = out


def _mlp_layer(x, norm_g, shift, scale, gate, w1, w2, final_g=None):
    bsz, seq, d = x.shape
    f = w1.shape[1]
    tl = TOKEN_BLOCK
    tok = pl.BlockSpec((1, tl, d), lambda b, i: (b, i, 0))
    per_b = pl.BlockSpec((1, 1, d), lambda b, i: (b, 0, 0))
    in_specs = [tok, _const_spec((1, d)), per_b, per_b, per_b,
                _const_spec((d, f)), _const_spec((f, d))]
    args = [x, norm_g.reshape(1, d), shift, scale, gate, w1.astype(BF16), w2.astype(BF16)]
    if final_g is not None:
        in_specs.append(_const_spec((1, d)))
        args.append(final_g.reshape(1, d))
    return pl.pallas_call(
        functools.partial(_mlp_kernel, final=final_g is not None),
        grid=(bsz, seq // tl),
        in_specs=in_specs,
        out_specs=tok,
        out_shape=jax.ShapeDtypeStruct(x.shape, F32),
        compiler_params=_compiler_params(
            ("parallel", "parallel"),
            2 * _nbytes((tl, d), F32) + _nbytes((d, f), BF16),
            _nbytes((tl, f), F32) * 2 + _nbytes((tl, d), F32) * 2),
        name="mlp",
    )(*args)


def _s5_in_kernel(x_ref, g_ref, sh_ref, sc_ref, wt_ref, u_ref):
    d = wt_ref.shape[0]
    for j in range(u_ref.shape[1]):
        h = _rms_mod(x_ref[0, :, j * d:(j + 1) * d], g_ref[...], sh_ref[0], sc_ref[0])
        u = lax.dot_general(wt_ref[...], h.astype(BF16), (((1,), (1,)), ((), ())),
                            preferred_element_type=F32)
        u_ref[0, j] = u.astype(BF16)


def _group_inputs(x_ref, gi):
    q, _, nc = x_ref.shape[1:]
    return x_ref[0, :, gi * S5_GROUP:(gi + 1) * S5_GROUP, :].reshape(q * S5_GROUP, nc)


def _s5_state_kernel(x_ref, ws_ref, s_ref):
    p = S5_STATE
    parts = [jnp.dot(ws_ref[gi], _group_inputs(x_ref, gi), preferred_element_type=F32)
             for gi in range(2)]
    st = jnp.concatenate([parts[0][:p], parts[1][:p], parts[0][p:], parts[1][p:]], axis=0)
    s_ref[...] = st.T


def _s5_scan_kernel(a_ref, s_ref, o_ref, h_ref):
    half = a_ref.shape[1] // 2

    @pl.when(pl.program_id(0) == 0)
    def _():
        h_ref[...] = jnp.zeros_like(h_ref)

    a_re = a_ref[:, :half]
    a_im = a_ref[:, half:]

    def body(c, carry):
        h_re, h_im = carry
        o_ref[c, :, :half] = h_re
        o_ref[c, :, half:] = h_im
        s = s_ref[c]
        return (a_re * h_re - a_im * h_im + s[:, :half],
                a_re * h_im + a_im * h_re + s[:, half:])

    h_re, h_im = lax.fori_loop(0, s_ref.shape[0], body,
                               (h_ref[:, :half], h_ref[:, half:]))
    h_ref[:, :half] = h_re
    h_ref[:, half:] = h_im


def _s5_y_kernel(x_ref, h_ref, t_ref, wy_ref, y_ref):
    q = x_ref.shape[1]
    nc = x_ref.shape[3]
    p = S5_STATE
    ht = h_ref[...].T.astype(BF16)
    for gi in range(2):
        hg = jnp.concatenate([ht[gi * p:(gi + 1) * p], ht[(2 + gi) * p:(3 + gi) * p]], axis=0)
        y = jnp.dot(t_ref[gi], _group_inputs(x_ref, gi), preferred_element_type=F32)
        y += jnp.dot(wy_ref[gi], hg, preferred_element_type=F32)
        y_ref[0, :, gi * S5_GROUP:(gi + 1) * S5_GROUP, :] = y.reshape(q, S5_GROUP, nc)


def _s5_out_kernel(y_ref, x_ref, w_ref, b_ref, gate_ref, o_ref):
    d = w_ref.shape[0]
    tg = y_ref.shape[1]
    g = jnp.concatenate([jax.nn.gelu(y_ref[0, j]).T for j in range(tg)], axis=0)
    ab = jnp.dot(g.astype(BF16), w_ref[...], preferred_element_type=F32) + b_ref[...]
    mix = ab[:, :d] * jax.nn.sigmoid(ab[:, d:])
    nc = mix.shape[0] // tg
    for j in range(tg):
        o_ref[0, :, j * d:(j + 1) * d] = (x_ref[0, :, j * d:(j + 1) * d]
                                          + gate_ref[0] * mix[j * nc:(j + 1) * nc])


def _s5_tables(lam_re, lam_im, log_dt, b_re, b_im, c_re, c_im, d_skip):
    g, p = lam_re.shape
    q, hh = S5_Q, S5_GROUP
    dt = jnp.exp(log_dt)[:, None]
    ld_re, ld_im = lam_re * dt, lam_im * dt

    def lam_pow(k):
        mag = jnp.exp(ld_re[..., None] * k)
        ang = ld_im[..., None] * k
        return mag * jnp.cos(ang), mag * jnp.sin(ang)

    pw_re, pw_im = lam_pow(jnp.arange(q + 1, dtype=F32))
    lb_re, lb_im = pw_re[..., 1], pw_im[..., 1]
    den = lam_re * lam_re + lam_im * lam_im
    f_re = ((lb_re - 1.0) * lam_re + lb_im * lam_im) / den
    f_im = (lb_im * lam_re - (lb_re - 1.0) * lam_im) / den
    bb_re = f_re[..., None] * b_re - f_im[..., None] * b_im
    bb_im = f_re[..., None] * b_im + f_im[..., None] * b_re

    kp_re = jnp.transpose(pw_re[..., :q], (0, 2, 1))[:, :, None, :]
    kp_im = jnp.transpose(pw_im[..., :q], (0, 2, 1))[:, :, None, :]
    cp_re = c_re[:, None] * kp_re - c_im[:, None] * kp_im
    cp_im = c_re[:, None] * kp_im + c_im[:, None] * kp_re
    lag = (jnp.einsum('gkop,gpi->gkoi', cp_re, bb_re, precision=HIGHEST)
           - jnp.einsum('gkop,gpi->gkoi', cp_im, bb_im, precision=HIGHEST))
    lag = lag.at[:, 0].add(d_skip.reshape(g, hh)[:, :, None] * jnp.eye(hh, dtype=F32))
    tt = jnp.arange(q)
    sub_diag = (tt[None, :, None] - tt[None, None, :] == tt[:, None, None]).astype(F32)
    toep = jnp.einsum('kut,gkoh->guoth', sub_diag, lag, precision=HIGHEST).reshape(g, q * hh, q * hh)

    rv_re, rv_im = pw_re[..., :q][..., ::-1], pw_im[..., :q][..., ::-1]
    ws_re = rv_re[..., None] * bb_re[:, :, None, :] - rv_im[..., None] * bb_im[:, :, None, :]
    ws_im = rv_re[..., None] * bb_im[:, :, None, :] + rv_im[..., None] * bb_re[:, :, None, :]
    ws = jnp.stack([ws_re, ws_im], axis=1).reshape(g, 2 * p, q * hh)

    up_re = jnp.transpose(pw_re[..., 1:], (0, 2, 1))[:, :, None, :]
    up_im = jnp.transpose(pw_im[..., 1:], (0, 2, 1))[:, :, None, :]
    cy_re = c_re[:, None] * up_re - c_im[:, None] * up_im
    cy_im = c_re[:, None] * up_im + c_im[:, None] * up_re
    wy = jnp.stack([cy_re, -cy_im], axis=3).reshape(g, q * hh, 2 * p)

    a_p = jnp.concatenate([pw_re[..., q].reshape(g // 2, 2 * p),
                           pw_im[..., q].reshape(g // 2, 2 * p)], axis=1)
    return toep.astype(BF16), ws.astype(BF16), wy.astype(BF16), a_p


def _s5_layer(x, norm_g, shift, scale, gate, w_in, lam_re, lam_im, log_dt,
              b_re, b_im, c_re, c_im, d_skip, w_glu, b_glu):
    bsz, seq, d = x.shape
    groups = d // S5_GROUP
    pairs = groups // 2
    nc = seq // S5_Q
    pair_rows = 2 * S5_GROUP
    per_b = pl.BlockSpec((1, 1, d), lambda b, i, k: (b, 0, 0))

    xv = x.reshape(bsz, nc, S5_Q * d)
    ncb = S5_CHUNK_ROWS

    def tok_spec(tg):
        return pl.BlockSpec((1, ncb, tg * d), lambda b, i, k: (b, i, k))

    def chan_spec(tg):
        return pl.BlockSpec((1, tg, d, ncb), lambda b, i, k: (b, k, 0, i))

    tg = S5_IN_OFFSETS
    ut = pl.pallas_call(
        _s5_in_kernel,
        grid=(bsz, nc // ncb, S5_Q // tg),
        in_specs=[tok_spec(tg), _const_spec((1, d)), per_b, per_b, _const_spec((d, d))],
        out_specs=chan_spec(tg),
        out_shape=jax.ShapeDtypeStruct((bsz, S5_Q, d, nc), BF16),
        compiler_params=_compiler_params(("parallel", "parallel", "parallel"),
                                         2 * _nbytes((ncb, tg * d), F32) + _nbytes((d, d), BF16),
                                         4 * _nbytes((ncb, d), F32)),
        name="s5_in",
    )(xv, norm_g.reshape(1, d), shift, scale, w_in.T.astype(BF16))

    toep, ws, wy, a_pair = _s5_tables(lam_re, lam_im, log_dt, b_re, b_im, c_re, c_im, d_skip)

    rows = bsz * pairs
    xspec = pl.BlockSpec((1, S5_Q, pair_rows, nc), lambda b, p: (b, 0, p, 0))
    sspec = pl.BlockSpec((nc, S5_PAIR_STATE), lambda b, p: (0, b * pairs + p))

    def table_spec(r, c):
        return pl.BlockSpec((2, r, c), lambda b, p: (p, 0, 0))

    s_loc = pl.pallas_call(
        _s5_state_kernel,
        grid=(bsz, pairs),
        in_specs=[xspec, table_spec(2 * S5_STATE, S5_QW)],
        out_specs=sspec,
        out_shape=jax.ShapeDtypeStruct((nc, rows * S5_PAIR_STATE), F32),
        compiler_params=_compiler_params(
            ("parallel", "parallel"),
            _nbytes((nc, 2 * S5_QW), BF16) + _nbytes((nc, S5_PAIR_STATE), F32),
            3 * _nbytes((nc, S5_PAIR_STATE), F32)),
        name="s5_chunk_state",
    )(ut, ws)

    cb = SCAN_BLOCK
    sblk = pl.BlockSpec((cb, rows, S5_PAIR_STATE), lambda i: (i, 0, 0))
    h_in = pl.pallas_call(
        _s5_scan_kernel,
        grid=(nc // cb,),
        in_specs=[_const_spec((rows, S5_PAIR_STATE)), sblk],
        out_specs=sblk,
        out_shape=jax.ShapeDtypeStruct((nc, rows, S5_PAIR_STATE), F32),
        scratch_shapes=[pltpu.VMEM((rows, S5_PAIR_STATE), F32)],
        compiler_params=_compiler_params(("arbitrary",),
                                         2 * _nbytes((cb, rows, S5_PAIR_STATE), F32),
                                         1024 * 1024),
        name="s5_chunk_scan",
    )(jnp.tile(a_pair, (bsz, 1)), s_loc.reshape(nc, rows, S5_PAIR_STATE))

    yt = pl.pallas_call(
        _s5_y_kernel,
        grid=(bsz, pairs),
        in_specs=[xspec, sspec, table_spec(S5_QW, S5_QW), table_spec(S5_QW, 2 * S5_STATE)],
        out_specs=xspec,
        out_shape=jax.ShapeDtypeStruct((bsz, S5_Q, d, nc), F32),
        compiler_params=_compiler_params(
            ("parallel", "parallel"),
            _nbytes((nc, 2 * S5_QW), BF16) + _nbytes((nc, 2 * S5_QW), F32)
            + _nbytes((nc, S5_PAIR_STATE), F32),
            3 * _nbytes((nc, 2 * S5_QW), F32)),
        name="s5_chunk_output",
    )(ut, h_in.reshape(nc, rows * S5_PAIR_STATE), toep, wy)

    tg = S5_OUT_OFFSETS
    out = pl.pallas_call(
        _s5_out_kernel,
        grid=(bsz, nc // ncb, S5_Q // tg),
        in_specs=[chan_spec(tg), tok_spec(tg), _const_spec((d, 2 * d)),
                  _const_spec((1, 2 * d)), per_b],
        out_specs=tok_spec(tg),
        out_shape=jax.ShapeDtypeStruct(xv.shape, F32),
        compiler_params=_compiler_params(("parallel", "parallel", "parallel"),
                                         3 * _nbytes((ncb, tg * d), F32) + _nbytes((d, 2 * d), BF16),
                                         6 * _nbytes((tg * ncb, 2 * d), F32)),
        name="s5_out",
    )(yt, xv, w_glu.astype(BF16), b_glu.reshape(1, 2 * d), gate)
    return out.reshape(bsz, seq, d)


def _ssd_kernel(x_ref, g_ref, sh_ref, sc_ref, gate_ref, wz_ref, wx_ref, wdtt_ref, cw_ref, cb_ref,
                biast_ref, alogt_ref, dskip_ref, ng_ref, wo_ref, o_ref,
                h_ref, xbc_ref, dtt_ref, y_ref, halo_ref, st_ref, *, d_inner, heads):
    q = M2_CHUNK
    n = M2_STATE
    hp = M2_HEAD_PAIR
    ph_n = M2_PHASES
    rpc = M2_PHASE_ROWS
    kc = M2_BLOCK_CHUNKS
    halo = V7X_SUBLANES
    d = x_ref.shape[2] // ph_n
    n_prev = M2_CONV - 1

    @pl.when(pl.program_id(1) == 0)
    def _():
        halo_ref[:, halo - 1:halo, :] = jnp.zeros((n_prev, 1, halo_ref.shape[2]), F32)
        st_ref[...] = jnp.zeros_like(st_ref)

    def piece(ref, k, ph):
        return ref[0, k * rpc:(k + 1) * rpc, ph * d:(ph + 1) * d]

    xb = jnp.concatenate([piece(x_ref, k, ph) for k in range(kc) for ph in range(ph_n)], axis=0)
    h = _rms_mod(xb, g_ref[...], sh_ref[0], sc_ref[0]).astype(BF16)
    h_ref[...] = h
    xr = jnp.dot(h, wx_ref[...], preferred_element_type=F32)
    dtt = _softplus(lax.dot_general(wdtt_ref[...], h, (((1,), (1,)), ((), ())),
                                    preferred_element_type=F32) + biast_ref[...])
    for k in range(kc):
        dtt_ref[k] = dtt[:, k * q:(k + 1) * q]

    for j in range(n_prev):
        ph = ph_n - n_prev + j
        for k in range(kc):
            halo_ref[j, halo + k * rpc:halo + (k + 1) * rpc, :] = (
                xr[k * q + ph * rpc:k * q + (ph + 1) * rpc, :])
    prev = [halo_ref[j, pl.ds(halo - 1, kc * rpc), :] for j in range(n_prev)]
    for j in range(n_prev):
        halo_ref[j, halo - 1:halo, :] = halo_ref[j, halo + kc * rpc - 1:halo + kc * rpc, :]
    for k in range(kc):
        xk = xr[k * q:(k + 1) * q, :]
        pk = jnp.concatenate([prev[j][k * rpc:(k + 1) * rpc] for j in range(n_prev)], axis=0)
        acc = cb_ref[...] + cw_ref[M2_CONV - 1:M2_CONV, :] * xk
        for s in range(1, M2_CONV):
            shifted = jnp.concatenate([pk[(n_prev - s) * rpc:], xk[:q - s * rpc]], axis=0)
            acc += cw_ref[M2_CONV - 1 - s:M2_CONV - s, :] * shifted
        xbc_ref[k] = acc * jax.nn.sigmoid(acc)

    shift_bits = rpc.bit_length() - 1
    row = lax.broadcasted_iota(jnp.int32, (q, q), 0)
    col = lax.broadcasted_iota(jnp.int32, (q, q), 1)
    pos_r = (row & (rpc - 1)) * ph_n + lax.shift_right_logical(row, shift_bits)
    pos_c = (col & (rpc - 1)) * ph_n + lax.shift_right_logical(col, shift_bits)
    causal = pos_r >= pos_c
    cum_mat = (pos_r <= pos_c).astype(F32)
    neg_a = -jnp.exp(alogt_ref[...])
    lane = lax.broadcasted_iota(jnp.int32, (q, hp), 1)
    first = lane < M2_HEADDIM
    first_row = lax.broadcasted_iota(jnp.int32, (1, hp), 1) < M2_HEADDIM
    heads_per_group = heads // M2_GROUPS
    pairs_per_group = heads_per_group // 2

    def chunk(k, carry):
        xbc = xbc_ref[k]
        dtk = dtt_ref[k]
        a_cst = jnp.dot(dtk * neg_a, cum_mat, precision=HIGHEST,
                        preferred_element_type=F32)
        a_cs = jnp.concatenate([a_cst, jnp.zeros((q - heads, q), F32)], axis=0).T
        state_w = jnp.exp(a_cst[:, q - 1:q] - a_cst) * dtk
        chunk_decay = jnp.exp(a_cs[q - 1:q, :])
        for g in range(M2_GROUPS):
            b_g = xbc[:, d_inner + g * n:d_inner + (g + 1) * n]
            c_g = xbc[:, d_inner + M2_GROUPS * n + g * n:d_inner + M2_GROUPS * n + (g + 1) * n]
            cb = lax.dot_general(c_g.astype(BF16), b_g.astype(BF16), (((1,), (1,)), ((), ())),
                                 preferred_element_type=F32)
            b_t = b_g.T
            for j in range(pairs_per_group):
                h0 = g * heads_per_group + 2 * j
                sl = slice((h0 // 2) * hp, (h0 // 2 + 1) * hp)
                lhs, lhs_state = [], []
                for hd in (h0, h0 + 1):
                    bc = jnp.broadcast_to(a_cs[:, hd:hd + 1], (q, q))
                    decay = jnp.exp(jnp.where(causal, bc - a_cst[hd:hd + 1, :], -jnp.inf))
                    lhs.append(cb * decay * dtk[hd:hd + 1, :])
                    lhs_state.append(b_t * state_w[hd:hd + 1, :])
                for hd in (h0, h0 + 1):
                    lhs.append(c_g * jnp.exp(jnp.broadcast_to(a_cs[:, hd:hd + 1], (q, n))))
                xs_p = xbc[:, sl]
                st_p = st_ref[:, sl]
                x0 = jnp.where(first, xs_p, 0.0)
                x1 = jnp.where(first, 0.0, xs_p)
                s0 = jnp.where(first, st_p, 0.0)
                s1 = jnp.where(first, 0.0, st_p)
                lhs = jnp.concatenate(lhs, axis=1).astype(BF16)
                rhs = jnp.concatenate([x0, x1, s0, s1], axis=0).astype(BF16)
                y = jnp.dot(lhs, rhs, preferred_element_type=F32)
                y_ref[k, :, sl] = y + xs_p * dskip_ref[:, sl]
                s_new = jnp.dot(jnp.concatenate(lhs_state, axis=1).astype(BF16),
                                jnp.concatenate([x0, x1], axis=0).astype(BF16),
                                preferred_element_type=F32)
                cd = jnp.where(first_row, chunk_decay[:, h0:h0 + 1], chunk_decay[:, h0 + 1:h0 + 2])
                st_ref[:, sl] = st_p * cd + s_new
        return carry

    lax.fori_loop(0, kc, chunk, 0)

    z = jnp.dot(h_ref[...], wz_ref[...], preferred_element_type=F32)
    y = y_ref[...].reshape(kc * q, d_inner) * (z * jax.nn.sigmoid(z))
    gw = d_inner // M2_GROUPS
    parts = []
    for g in range(M2_GROUPS):
        yg = y[:, g * gw:(g + 1) * gw]
        parts.append(yg * lax.rsqrt(jnp.mean(yg * yg, axis=-1, keepdims=True) + NORM_EPS))
    y = jnp.concatenate(parts, axis=1) * ng_ref[...]
    out = jnp.dot(y.astype(BF16), wo_ref[...], preferred_element_type=F32)
    for k in range(kc):
        for ph in range(ph_n):
            r0 = k * q + ph * rpc
            o_ref[0, k * rpc:(k + 1) * rpc, ph * d:(ph + 1) * d] = (
                piece(x_ref, k, ph) + gate_ref[0] * out[r0:r0 + rpc])


def _ssd_layer(x, norm_g, shift, scale, gate, w_in, conv_w, conv_b, dt_bias, a_log,
               d_skip, gn_g, w_out):
    bsz, seq, d = x.shape
    d_inner = w_out.shape[0]
    heads = d_inner // M2_HEADDIM
    conv_dim = d_inner + 2 * M2_GROUPS * M2_STATE
    q = M2_CHUNK
    kc = M2_BLOCK_CHUNKS
    rows = kc * M2_PHASE_ROWS

    w_z = w_in[:, :d_inner].astype(BF16)
    w_x = w_in[:, d_inner:d_inner + conv_dim].astype(BF16)
    w_dt_t = w_in[:, d_inner + conv_dim:].T.astype(BF16)

    xv = x.reshape(bsz, seq // M2_PHASES, M2_PHASES * d)
    blk = pl.BlockSpec((1, rows, M2_PHASES * d), lambda b, i: (b, i, 0))
    per_b = pl.BlockSpec((1, 1, d), lambda b, i: (b, 0, 0))
    out = pl.pallas_call(
        functools.partial(_ssd_kernel, d_inner=d_inner, heads=heads),
        grid=(bsz, seq // (kc * q)),
        in_specs=[blk, _const_spec((1, d)), per_b, per_b, per_b,
                  _const_spec((d, d_inner)), _const_spec((d, conv_dim)), _const_spec((heads, d)),
                  _const_spec((M2_CONV, conv_dim)), _const_spec((1, conv_dim)),
                  _const_spec((heads, 1)), _const_spec((heads, 1)),
                  _const_spec((1, d_inner)), _const_spec((1, d_inner)), _const_spec((d_inner, d))],
        out_specs=blk,
        out_shape=jax.ShapeDtypeStruct(xv.shape, F32),
        scratch_shapes=[pltpu.VMEM((kc * q, d), BF16),
                        pltpu.VMEM((kc, q, conv_dim), F32),
                        pltpu.VMEM((kc, heads, q), F32),
                        pltpu.VMEM((kc, q, d_inner), F32),
                        pltpu.VMEM((M2_CONV - 1, V7X_SUBLANES + rows, conv_dim), F32),
                        pltpu.VMEM((M2_STATE, d_inner), F32)],
        compiler_params=_compiler_params(
            ("arbitrary", "arbitrary"),
            2 * _nbytes((kc * q, d), F32),
            _nbytes((d, 2 * d_inner + conv_dim), BF16)
            + 2 * _nbytes((kc * q, conv_dim + d_inner), F32)
            + 2 * _nbytes((kc * q, conv_dim), F32)),
        name="ssd_mixer",
    )(xv, norm_g.reshape(1, d), shift, scale, gate, w_z, w_x, w_dt_t,
      conv_w, conv_b.reshape(1, conv_dim), dt_bias.reshape(heads, 1), a_log.reshape(heads, 1),
      jnp.repeat(d_skip, M2_HEADDIM).reshape(1, d_inner), gn_g.reshape(1, d_inner),
      w_out.astype(BF16))
    return out.reshape(bsz, seq, d)


def kernel(x, c, ada_w, ada_b, norm_mix_g, norm_mlp_g, mlp_w1, mlp_w2, s5_w_in, s5_lambda_re, s5_lambda_im, s5_log_dt, s5_b_re, s5_b_im, s5_c_re, s5_c_im, s5_d, s5_w_glu, s5_b_glu, m2_w_in, m2_conv_w, m2_conv_b, m2_dt_bias, m2_a_log, m2_d, m2_norm_g, m2_w_out, final_norm_g):
    depth = ada_w.shape[0]
    bsz, _, d = x.shape
    mod = _ada_modulation(c, ada_w, ada_b).reshape(depth, bsz, N_MOD, 1, d)
    for i in range(depth):
        sh1, sc1, g1, sh2, sc2, g2 = (mod[i, :, k] for k in range(N_MOD))
        j = i // 2
        if i % 2 == 0:
            x = _s5_layer(x, norm_mix_g[i], sh1, sc1, g1, s5_w_in[j], s5_lambda_re[j],
                          s5_lambda_im[j], s5_log_dt[j], s5_b_re[j], s5_b_im[j],
                          s5_c_re[j], s5_c_im[j], s5_d[j], s5_w_glu[j], s5_b_glu[j])
        else:
            x = _ssd_layer(x, norm_mix_g[i], sh1, sc1, g1, m2_w_in[j], m2_conv_w[j],
                           m2_conv_b[j], m2_dt_bias[j], m2_a_log[j], m2_d[j],
                           m2_norm_g[j], m2_w_out[j])
        x = _mlp_layer(x, norm_mlp_g[i], sh2, sc2, g2, mlp_w1[i], mlp_w2[i],
                       final_g=final_norm_g if i == depth - 1 else None)
    return x
```

```python
import functools

import jax
import jax.numpy as jnp
from jax import lax
from jax.experimental import pallas as pl
from jax.experimental.pallas import tpu as pltpu

F32 = jnp.float32
BF16 = jnp.bfloat16
HIGHEST = lax.Precision.HIGHEST
NORM_EPS = 1e-5
N_MOD = 6
LOG2_E = 1.4426950408889634

V7X_LANES = 128
V7X_SUBLANES = 8
V7X_SCOPED_VMEM_BYTES = 60000 * 1024

S5_GROUP = 16
S5_STATE = 64
S5_Q = 16
S5_QW = S5_Q * S5_GROUP
S5_PAIR_STATE = 4 * S5_STATE
S5_IN_CHUNKS = 256
S5_OUT_CHUNKS = 128
S5_OUT_OFFSETS = 4

M2_HEADDIM = 64
M2_GROUPS = 4
M2_STATE = 128
M2_CONV = 4
M2_CHUNK = 128
M2_HEAD_PAIR = 2 * M2_HEADDIM
M2_PHASES = V7X_SUBLANES
M2_PHASE_ROWS = M2_CHUNK // M2_PHASES
M2_BLOCK_CHUNKS = 4

TOKEN_BLOCK = 512
SCAN_BLOCK = 64


def _compiler_params(semantics, block_bytes, temp_bytes):
    want = 2 * block_bytes + temp_bytes
    return pltpu.CompilerParams(
        dimension_semantics=semantics,
        vmem_limit_bytes=int(min(V7X_SCOPED_VMEM_BYTES, max(want, 16 * 1024 * 1024))))


def _nbytes(shape, dtype):
    n = 1
    for s in shape:
        n *= s
    return n * jnp.dtype(dtype).itemsize


def _const_spec(shape):
    return pl.BlockSpec(shape, lambda *_: (0,) * len(shape), pipeline_mode=pl.Buffered(1))


def _rms_mod(x, g, shift, scale):
    y = x * lax.rsqrt(jnp.mean(x * x, axis=-1, keepdims=True) + NORM_EPS)
    return (y * g) * (1.0 + scale) + shift


def _softplus(x):
    return jnp.maximum(x, 0.0) + jnp.log1p(jnp.exp(-jnp.abs(x)))


def _ada_kernel(c_ref, w_ref, b_ref, o_ref):
    cond = jax.nn.silu(c_ref[...])
    o_ref[0] = jnp.dot(cond, w_ref[0], precision=HIGHEST,
                       preferred_element_type=F32) + b_ref[0]


def _ada_modulation(c, ada_w, ada_b):
    depth, d, n = ada_w.shape
    bsz = c.shape[0]
    rows = V7X_SUBLANES
    nb = n // 4
    c_pad = jnp.zeros((rows, d), F32).at[:bsz].set(c)
    out = pl.pallas_call(
        _ada_kernel,
        grid=(depth, n // nb),
        in_specs=[pl.BlockSpec((rows, d), lambda i, j: (0, 0)),
                  pl.BlockSpec((1, d, nb), lambda i, j: (i, 0, j)),
                  pl.BlockSpec((1, 1, nb), lambda i, j: (i, 0, j))],
        out_specs=pl.BlockSpec((1, rows, nb), lambda i, j: (i, 0, j)),
        out_shape=jax.ShapeDtypeStruct((depth, rows, n), F32),
        compiler_params=_compiler_params(("arbitrary", "arbitrary"),
                                         _nbytes((d, nb), F32), 4 * 1024 * 1024),
        name="ada_modulation",
    )(c_pad, ada_w, ada_b.reshape(depth, 1, n))
    return out[:, :bsz, :]


def _mlp_kernel(x_ref, g_ref, sh_ref, sc_ref, gate_ref, w1_ref, w2_ref, *rest, final):
    if final:
        fg_ref, o_ref = rest
    else:
        (o_ref,) = rest
    x = x_ref[0]
    h = _rms_mod(x, g_ref[...], sh_ref[0], sc_ref[0])
    a = jnp.maximum(jnp.dot(h.astype(BF16), w1_ref[...], preferred_element_type=F32), 0.0)
    y = jnp.dot((a * a).astype(BF16), w2_ref[...], preferred_element_type=F32)
    out = x + gate_ref[0] * y
    if final:
        out = out * lax.rsqrt(jnp.mean(out * out, axis=-1, keepdims=True) + NORM_EPS)
        out = out * fg_ref[...]
    o_ref[0] = out


def _mlp_layer(x, norm_g, shift, scale, gate, w1, w2, final_g=None):
    bsz, seq, d = x.shape
    f = w1.shape[1]
    tl = TOKEN_BLOCK
    tok = pl.BlockSpec((1, tl, d), lambda b, i: (b, i, 0))
    per_b = pl.BlockSpec((1, 1, d), lambda b, i: (b, 0, 0))
    in_specs = [tok, _const_spec((1, d)), per_b, per_b, per_b,
                _const_spec((d, f)), _const_spec((f, d))]
    args = [x, norm_g.reshape(1, d), shift, scale, gate, w1.astype(BF16), w2.astype(BF16)]
    if final_g is not None:
        in_specs.append(_const_spec((1, d)))
        args.append(final_g.reshape(1, d))
    return pl.pallas_call(
        functools.partial(_mlp_kernel, final=final_g is not None),
        grid=(bsz, seq // tl),
        in_specs=in_specs,
        out_specs=tok,
        out_shape=jax.ShapeDtypeStruct(x.shape, F32),
        compiler_params=_compiler_params(
            ("parallel", "parallel"),
            2 * _nbytes((tl, d), F32) + _nbytes((d, f), BF16),
            _nbytes((tl, f), F32) * 2 + _nbytes((tl, d), F32) * 2),
        name="mlp",
    )(*args)


def _s5_in_kernel(x_ref, g_ref, sh_ref, sc_ref, wt_ref, u_ref):
    for j in range(u_ref.shape[1]):
        h = _rms_mod(x_ref[0, :, j, :], g_ref[...], sh_ref[0], sc_ref[0])
        u = lax.dot_general(wt_ref[...], h.astype(BF16), (((1,), (1,)), ((), ())),
                            preferred_element_type=F32)
        u_ref[0, j] = u.astype(BF16)


def _group_inputs(x_ref, gi):
    q, _, nc = x_ref.shape[1:]
    return x_ref[0, :, gi * S5_GROUP:(gi + 1) * S5_GROUP, :].reshape(q * S5_GROUP, nc)


def _s5_state_kernel(x_ref, ws_ref, s_ref):
    p = S5_STATE
    parts = [jnp.dot(ws_ref[gi], _group_inputs(x_ref, gi), preferred_element_type=F32)
             for gi in range(2)]
    st = jnp.concatenate([parts[0][:p], parts[1][:p], parts[0][p:], parts[1][p:]], axis=0)
    s_ref[...] = st.T


def _s5_scan_kernel(a_ref, s_ref, o_ref, h_ref):
    half = a_ref.shape[1] // 2

    @pl.when(pl.program_id(0) == 0)
    def _():
        h_ref[...] = jnp.zeros_like(h_ref)

    a_re = a_ref[:, :half]
    a_im = a_ref[:, half:]

    def body(c, carry):
        h_re, h_im = carry
        o_ref[c, :, :half] = h_re
        o_ref[c, :, half:] = h_im
        s = s_ref[c]
        return (a_re * h_re - a_im * h_im + s[:, :half],
                a_re * h_im + a_im * h_re + s[:, half:])

    h_re, h_im = lax.fori_loop(0, s_ref.shape[0], body,
                               (h_ref[:, :half], h_ref[:, half:]))
    h_ref[:, :half] = h_re
    h_ref[:, half:] = h_im


def _s5_y_kernel(x_ref, h_ref, t_ref, wy_ref, y_ref):
    q = x_ref.shape[1]
    nc = x_ref.shape[3]
    p = S5_STATE
    ht = h_ref[...].T.astype(BF16)
    for gi in range(2):
        hg = jnp.concatenate([ht[gi * p:(gi + 1) * p], ht[(2 + gi) * p:(3 + gi) * p]], axis=0)
        y = jnp.dot(t_ref[gi], _group_inputs(x_ref, gi), preferred_element_type=F32)
        y += jnp.dot(wy_ref[gi], hg, preferred_element_type=F32)
        y_ref[0, :, gi * S5_GROUP:(gi + 1) * S5_GROUP, :] = y.reshape(q, S5_GROUP, nc)


def _s5_out_kernel(y_ref, x_ref, w_ref, b_ref, gate_ref, o_ref):
    d = w_ref.shape[0]
    nc = y_ref.shape[3]
    tg = S5_OUT_OFFSETS
    for j0 in range(0, y_ref.shape[1], tg):
        g = jnp.concatenate([jax.nn.gelu(y_ref[0, j0 + j]).T for j in range(tg)], axis=0)
        ab = jnp.dot(g.astype(BF16), w_ref[...], preferred_element_type=F32) + b_ref[...]
        mix = ab[:, :d] * jax.nn.sigmoid(ab[:, d:])
        for j in range(tg):
            o_ref[0, :, j0 + j, :] = (x_ref[0, :, j0 + j, :]
                                      + gate_ref[0] * mix[j * nc:(j + 1) * nc])


def _s5_tables(lam_re, lam_im, log_dt, b_re, b_im, c_re, c_im, d_skip):
    g, p = lam_re.shape
    q, hh = S5_Q, S5_GROUP
    dt = jnp.exp(log_dt)[:, None]
    ld_re, ld_im = lam_re * dt, lam_im * dt

    def lam_pow(k):
        mag = jnp.exp(ld_re[..., None] * k)
        ang = ld_im[..., None] * k
        return mag * jnp.cos(ang), mag * jnp.sin(ang)

    pw_re, pw_im = lam_pow(jnp.arange(q + 1, dtype=F32))
    lb_re, lb_im = pw_re[..., 1], pw_im[..., 1]
    den = lam_re * lam_re + lam_im * lam_im
    f_re = ((lb_re - 1.0) * lam_re + lb_im * lam_im) / den
    f_im = (lb_im * lam_re - (lb_re - 1.0) * lam_im) / den
    bb_re = f_re[..., None] * b_re - f_im[..., None] * b_im
    bb_im = f_re[..., None] * b_im + f_im[..., None] * b_re

    kp_re = jnp.transpose(pw_re[..., :q], (0, 2, 1))[:, :, None, :]
    kp_im = jnp.transpose(pw_im[..., :q], (0, 2, 1))[:, :, None, :]
    cp_re = c_re[:, None] * kp_re - c_im[:, None] * kp_im
    cp_im = c_re[:, None] * kp_im + c_im[:, None] * kp_re
    lag = (jnp.einsum('gkop,gpi->gkoi', cp_re, bb_re, precision=HIGHEST)
           - jnp.einsum('gkop,gpi->gkoi', cp_im, bb_im, precision=HIGHEST))
    lag = lag.at[:, 0].add(d_skip.reshape(g, hh)[:, :, None] * jnp.eye(hh, dtype=F32))
    tt = jnp.arange(q)
    sub_diag = (tt[None, :, None] - tt[None, None, :] == tt[:, None, None]).astype(F32)
    toep = jnp.einsum('kut,gkoh->guoth', sub_diag, lag, precision=HIGHEST).reshape(g, q * hh, q * hh)

    rv_re, rv_im = pw_re[..., :q][..., ::-1], pw_im[..., :q][..., ::-1]
    ws_re = rv_re[..., None] * bb_re[:, :, None, :] - rv_im[..., None] * bb_im[:, :, None, :]
    ws_im = rv_re[..., None] * bb_im[:, :, None, :] + rv_im[..., None] * bb_re[:, :, None, :]
    ws = jnp.stack([ws_re, ws_im], axis=1).reshape(g, 2 * p, q * hh)

    up_re = jnp.transpose(pw_re[..., 1:], (0, 2, 1))[:, :, None, :]
    up_im = jnp.transpose(pw_im[..., 1:], (0, 2, 1))[:, :, None, :]
    cy_re = c_re[:, None] * up_re - c_im[:, None] * up_im
    cy_im = c_re[:, None] * up_im + c_im[:, None] * up_re
    wy = jnp.stack([cy_re, -cy_im], axis=3).reshape(g, q * hh, 2 * p)

    a_p = jnp.concatenate([pw_re[..., q].reshape(g // 2, 2 * p),
                           pw_im[..., q].reshape(g // 2, 2 * p)], axis=1)
    return toep.astype(BF16), ws.astype(BF16), wy.astype(BF16), a_p


def _s5_layer(x, norm_g, shift, scale, gate, w_in, lam_re, lam_im, log_dt,
              b_re, b_im, c_re, c_im, d_skip, w_glu, b_glu):
    bsz, seq, d = x.shape
    groups = d // S5_GROUP
    pairs = groups // 2
    nc = seq // S5_Q
    pair_rows = 2 * S5_GROUP
    per_b = pl.BlockSpec((1, 1, d), lambda b, i, k: (b, 0, 0))

    xv = x.reshape(bsz, nc, S5_Q, d)
    tg = V7X_SUBLANES

    def tok_spec(ncb):
        return pl.BlockSpec((1, ncb, tg, d), lambda b, i, k: (b, i, k, 0))

    def chan_spec(ncb):
        return pl.BlockSpec((1, tg, d, ncb), lambda b, i, k: (b, k, 0, i))

    ncb = S5_IN_CHUNKS
    ut = pl.pallas_call(
        _s5_in_kernel,
        grid=(bsz, nc // ncb, S5_Q // tg),
        in_specs=[tok_spec(ncb), _const_spec((1, d)), per_b, per_b, _const_spec((d, d))],
        out_specs=chan_spec(ncb),
        out_shape=jax.ShapeDtypeStruct((bsz, S5_Q, d, nc), BF16),
        compiler_params=_compiler_params(("parallel", "parallel", "parallel"),
                                         _nbytes((ncb, tg, d), F32) + _nbytes((tg, d, ncb), BF16),
                                         _nbytes((d, d), BF16) + 4 * _nbytes((ncb, d), F32)),
        name="s5_in",
    )(xv, norm_g.reshape(1, d), shift, scale, w_in.T.astype(BF16))

    toep, ws, wy, a_pair = _s5_tables(lam_re, lam_im, log_dt, b_re, b_im, c_re, c_im, d_skip)

    rows = bsz * pairs
    xspec = pl.BlockSpec((1, S5_Q, pair_rows, nc), lambda b, p: (b, 0, p, 0))
    sspec = pl.BlockSpec((nc, S5_PAIR_STATE), lambda b, p: (0, b * pairs + p))

    def table_spec(r, c):
        return pl.BlockSpec((2, r, c), lambda b, p: (p, 0, 0))

    s_loc = pl.pallas_call(
        _s5_state_kernel,
        grid=(bsz, pairs),
        in_specs=[xspec, table_spec(2 * S5_STATE, S5_QW)],
        out_specs=sspec,
        out_shape=jax.ShapeDtypeStruct((nc, rows * S5_PAIR_STATE), F32),
        compiler_params=_compiler_params(
            ("parallel", "parallel"),
            _nbytes((nc, 2 * S5_QW), BF16) + _nbytes((nc, S5_PAIR_STATE), F32),
            3 * _nbytes((nc, S5_PAIR_STATE), F32)),
        name="s5_chunk_state",
    )(ut, ws)

    cb = SCAN_BLOCK
    sblk = pl.BlockSpec((cb, rows, S5_PAIR_STATE), lambda i: (i, 0, 0))
    h_in = pl.pallas_call(
        _s5_scan_kernel,
        grid=(nc // cb,),
        in_specs=[_const_spec((rows, S5_PAIR_STATE)), sblk],
        out_specs=sblk,
        out_shape=jax.ShapeDtypeStruct((nc, rows, S5_PAIR_STATE), F32),
        scratch_shapes=[pltpu.VMEM((rows, S5_PAIR_STATE), F32)],
        compiler_params=_compiler_params(("arbitrary",),
                                         2 * _nbytes((cb, rows, S5_PAIR_STATE), F32),
                                         1024 * 1024),
        name="s5_chunk_scan",
    )(jnp.tile(a_pair, (bsz, 1)), s_loc.reshape(nc, rows, S5_PAIR_STATE))

    yt = pl.pallas_call(
        _s5_y_kernel,
        grid=(bsz, pairs),
        in_specs=[xspec, sspec, table_spec(S5_QW, S5_QW), table_spec(S5_QW, 2 * S5_STATE)],
        out_specs=xspec,
        out_shape=jax.ShapeDtypeStruct((bsz, S5_Q, d, nc), F32),
        compiler_params=_compiler_params(
            ("parallel", "parallel"),
            _nbytes((nc, 2 * S5_QW), BF16) + _nbytes((nc, 2 * S5_QW), F32)
            + _nbytes((nc, S5_PAIR_STATE), F32),
            3 * _nbytes((nc, 2 * S5_QW), F32)),
        name="s5_chunk_output",
    )(ut, h_in.reshape(nc, rows * S5_PAIR_STATE), toep, wy)

    ncb = S5_OUT_CHUNKS
    out = pl.pallas_call(
        _s5_out_kernel,
        grid=(bsz, nc // ncb, S5_Q // tg),
        in_specs=[chan_spec(ncb), tok_spec(ncb), _const_spec((d, 2 * d)),
                  _const_spec((1, 2 * d)), per_b],
        out_specs=tok_spec(ncb),
        out_shape=jax.ShapeDtypeStruct(xv.shape, F32),
        compiler_params=_compiler_params(
            ("parallel", "parallel", "parallel"),
            3 * _nbytes((ncb, tg, d), F32),
            _nbytes((d, 2 * d), BF16) + 6 * _nbytes((S5_OUT_OFFSETS * ncb, 2 * d), F32)),
        name="s5_out",
    )(yt, xv, w_glu.astype(BF16), b_glu.reshape(1, 2 * d), gate)
    return out.reshape(bsz, seq, d)


def _ssd_kernel(x_ref, g_ref, sh_ref, sc_ref, gate_ref, wz_ref, wx_ref, wdtt_ref, cw_ref, cb_ref,
                biast_ref, alogt_ref, dskip_ref, ng_ref, wo_ref, o_ref,
                h_ref, xbc_ref, dtt_ref, y_ref, halo_ref, st_ref, *, d_inner, heads):
    q = M2_CHUNK
    n = M2_STATE
    hp = M2_HEAD_PAIR
    ph_n = M2_PHASES
    rpc = M2_PHASE_ROWS
    kc = M2_BLOCK_CHUNKS
    halo = V7X_SUBLANES
    n_prev = M2_CONV - 1

    @pl.when(pl.program_id(1) == 0)
    def _():
        halo_ref[:, halo - 1:halo, :] = jnp.zeros((n_prev, 1, halo_ref.shape[2]), F32)
        st_ref[...] = jnp.zeros_like(st_ref)

    def rows_of(k):
        return slice(k * rpc, (k + 1) * rpc)

    xb = jnp.concatenate([x_ref[0, rows_of(k), ph, :] for k in range(kc) for ph in range(ph_n)],
                         axis=0)
    h = _rms_mod(xb, g_ref[...], sh_ref[0], sc_ref[0]).astype(BF16)
    h_ref[...] = h
    xr = jnp.dot(h, wx_ref[...], preferred_element_type=F32)
    dtt = _softplus(lax.dot_general(wdtt_ref[...], h, (((1,), (1,)), ((), ())),
                                    preferred_element_type=F32) + biast_ref[...])
    for k in range(kc):
        dtt_ref[k] = dtt[:, k * q:(k + 1) * q]

    for j in range(n_prev):
        ph = ph_n - n_prev + j
        for k in range(kc):
            halo_ref[j, halo + k * rpc:halo + (k + 1) * rpc, :] = (
                xr[k * q + ph * rpc:k * q + (ph + 1) * rpc, :])
    prev = [halo_ref[j, pl.ds(halo - 1, kc * rpc), :] for j in range(n_prev)]
    for j in range(n_prev):
        halo_ref[j, halo - 1:halo, :] = halo_ref[j, halo + kc * rpc - 1:halo + kc * rpc, :]
    for k in range(kc):
        xk = xr[k * q:(k + 1) * q, :]
        pk = jnp.concatenate([prev[j][k * rpc:(k + 1) * rpc] for j in range(n_prev)], axis=0)
        acc = cb_ref[...] + cw_ref[M2_CONV - 1:M2_CONV, :] * xk
        for s in range(1, M2_CONV):
            shifted = jnp.concatenate([pk[(n_prev - s) * rpc:], xk[:q - s * rpc]], axis=0)
            acc += cw_ref[M2_CONV - 1 - s:M2_CONV - s, :] * shifted
        xbc_ref[k] = acc * jax.nn.sigmoid(acc)

    shift_bits = rpc.bit_length() - 1
    row = lax.broadcasted_iota(jnp.int32, (q, q), 0)
    col = lax.broadcasted_iota(jnp.int32, (q, q), 1)
    pos_r = (row & (rpc - 1)) * ph_n + lax.shift_right_logical(row, shift_bits)
    pos_c = (col & (rpc - 1)) * ph_n + lax.shift_right_logical(col, shift_bits)
    causal = pos_r >= pos_c
    cum_mat = (pos_r <= pos_c).astype(F32)
    neg_a = -jnp.exp(alogt_ref[...])
    lane = lax.broadcasted_iota(jnp.int32, (q, hp), 1)
    first = lane < M2_HEADDIM
    first_row = lax.broadcasted_iota(jnp.int32, (1, hp), 1) < M2_HEADDIM
    heads_per_group = heads // M2_GROUPS
    pairs_per_group = heads_per_group // 2

    def chunk(k, carry):
        xbc = xbc_ref[k]
        dtk = dtt_ref[k]
        a_cst = LOG2_E * jnp.dot(dtk * neg_a, cum_mat, precision=HIGHEST,
                                 preferred_element_type=F32)
        src = a_cst - LOG2_E * jnp.log(dtk)
        a_cs = jnp.concatenate([a_cst, jnp.zeros((q - heads, q), F32)], axis=0).T
        state_w = jnp.exp2(a_cst[:, q - 1:q] - a_cst) * dtk
        chunk_decay = jnp.exp2(a_cs[q - 1:q, :])
        for g in range(M2_GROUPS):
            b_g = xbc[:, d_inner + g * n:d_inner + (g + 1) * n]
            c_g = xbc[:, d_inner + M2_GROUPS * n + g * n:d_inner + M2_GROUPS * n + (g + 1) * n]
            cb = lax.dot_general(c_g.astype(BF16), b_g.astype(BF16), (((1,), (1,)), ((), ())),
                                 preferred_element_type=F32)
            b_t = b_g.T
            for j in range(pairs_per_group):
                h0 = g * heads_per_group + 2 * j
                sl = slice((h0 // 2) * hp, (h0 // 2 + 1) * hp)
                lhs, lhs_state = [], []
                for hd in (h0, h0 + 1):
                    bc = jnp.broadcast_to(a_cs[:, hd:hd + 1], (q, q))
                    decay_dt = jnp.exp2(jnp.where(causal, bc - src[hd:hd + 1, :], -jnp.inf))
                    lhs.append(cb * decay_dt)
                    lhs_state.append(b_t * state_w[hd:hd + 1, :])
                for hd in (h0, h0 + 1):
                    lhs.append(c_g * jnp.exp2(jnp.broadcast_to(a_cs[:, hd:hd + 1], (q, n))))
                xs_p = xbc[:, sl]
                st_p = st_ref[:, sl]
                x0 = jnp.where(first, xs_p, 0.0)
                x1 = jnp.where(first, 0.0, xs_p)
                s0 = jnp.where(first, st_p, 0.0)
                s1 = jnp.where(first, 0.0, st_p)
                lhs = jnp.concatenate(lhs, axis=1).astype(BF16)
                rhs = jnp.concatenate([x0, x1, s0, s1], axis=0).astype(BF16)
                y = jnp.dot(lhs, rhs, preferred_element_type=F32)
                y_ref[k, :, sl] = y + xs_p * dskip_ref[:, sl]
                s_new = jnp.dot(jnp.concatenate(lhs_state, axis=1).astype(BF16),
                                jnp.concatenate([x0, x1], axis=0).astype(BF16),
                                preferred_element_type=F32)
                cd = jnp.where(first_row, chunk_decay[:, h0:h0 + 1], chunk_decay[:, h0 + 1:h0 + 2])
                st_ref[:, sl] = st_p * cd + s_new
        return carry

    lax.fori_loop(0, kc, chunk, 0)

    z = jnp.dot(h_ref[...], wz_ref[...], preferred_element_type=F32)
    y = y_ref[...].reshape(kc * q, d_inner) * (z * jax.nn.sigmoid(z))
    gw = d_inner // M2_GROUPS
    parts = []
    for g in range(M2_GROUPS):
        yg = y[:, g * gw:(g + 1) * gw]
        parts.append(yg * lax.rsqrt(jnp.mean(yg * yg, axis=-1, keepdims=True) + NORM_EPS))
    y = jnp.concatenate(parts, axis=1) * ng_ref[...]
    out = jnp.dot(y.astype(BF16), wo_ref[...], preferred_element_type=F32)
    for k in range(kc):
        for ph in range(ph_n):
            r0 = k * q + ph * rpc
            o_ref[0, rows_of(k), ph, :] = (x_ref[0, rows_of(k), ph, :]
                                           + gate_ref[0] * out[r0:r0 + rpc])


def _ssd_layer(x, norm_g, shift, scale, gate, w_in, conv_w, conv_b, dt_bias, a_log,
               d_skip, gn_g, w_out):
    bsz, seq, d = x.shape
    d_inner = w_out.shape[0]
    heads = d_inner // M2_HEADDIM
    conv_dim = d_inner + 2 * M2_GROUPS * M2_STATE
    q = M2_CHUNK
    kc = M2_BLOCK_CHUNKS
    rows = kc * M2_PHASE_ROWS

    w_z = w_in[:, :d_inner].astype(BF16)
    w_x = w_in[:, d_inner:d_inner + conv_dim].astype(BF16)
    w_dt_t = w_in[:, d_inner + conv_dim:].T.astype(BF16)

    xv = x.reshape(bsz, seq // M2_PHASES, M2_PHASES, d)
    blk = pl.BlockSpec((1, rows, M2_PHASES, d), lambda b, i: (b, i, 0, 0))
    per_b = pl.BlockSpec((1, 1, d), lambda b, i: (b, 0, 0))
    out = pl.pallas_call(
        functools.partial(_ssd_kernel, d_inner=d_inner, heads=heads),
        grid=(bsz, seq // (kc * q)),
        in_specs=[blk, _const_spec((1, d)), per_b, per_b, per_b,
                  _const_spec((d, d_inner)), _const_spec((d, conv_dim)), _const_spec((heads, d)),
                  _const_spec((M2_CONV, conv_dim)), _const_spec((1, conv_dim)),
                  _const_spec((heads, 1)), _const_spec((heads, 1)),
                  _const_spec((1, d_inner)), _const_spec((1, d_inner)), _const_spec((d_inner, d))],
        out_specs=blk,
        out_shape=jax.ShapeDtypeStruct(xv.shape, F32),
        scratch_shapes=[pltpu.VMEM((kc * q, d), BF16),
                        pltpu.VMEM((kc, q, conv_dim), F32),
                        pltpu.VMEM((kc, heads, q), F32),
                        pltpu.VMEM((kc, q, d_inner), F32),
                        pltpu.VMEM((M2_CONV - 1, V7X_SUBLANES + rows, conv_dim), F32),
                        pltpu.VMEM((M2_STATE, d_inner), F32)],
        compiler_params=_compiler_params(
            ("arbitrary", "arbitrary"),
            2 * _nbytes((kc * q, d), F32),
            _nbytes((d, 2 * d_inner + conv_dim), BF16)
            + 2 * _nbytes((kc * q, conv_dim + d_inner), F32)
            + 2 * _nbytes((kc * q, conv_dim), F32)),
        name="ssd_mixer",
    )(xv, norm_g.reshape(1, d), shift, scale, gate, w_z, w_x, w_dt_t,
      conv_w, conv_b.reshape(1, conv_dim), dt_bias.reshape(heads, 1), a_log.reshape(heads, 1),
      jnp.repeat(d_skip, M2_HEADDIM).reshape(1, d_inner), gn_g.reshape(1, d_inner),
      w_out.astype(BF16))
    return out.reshape(bsz, seq, d)


def kernel(x, c, ada_w, ada_b, norm_mix_g, norm_mlp_g, mlp_w1, mlp_w2, s5_w_in, s5_lambda_re, s5_lambda_im, s5_log_dt, s5_b_re, s5_b_im, s5_c_re, s5_c_im, s5_d, s5_w_glu, s5_b_glu, m2_w_in, m2_conv_w, m2_conv_b, m2_dt_bias, m2_a_log, m2_d, m2_norm_g, m2_w_out, final_norm_g):
    depth = ada_w.shape[0]
    bsz, _, d = x.shape
    mod = _ada_modulation(c, ada_w, ada_b).reshape(depth, bsz, N_MOD, 1, d)
    for i in range(depth):
        sh1, sc1, g1, sh2, sc2, g2 = (mod[i, :, k] for k in range(N_MOD))
        j = i // 2
        if i % 2 == 0:
            x = _s5_layer(x, norm_mix_g[i], sh1, sc1, g1, s5_w_in[j], s5_lambda_re[j],
                          s5_lambda_im[j], s5_log_dt[j], s5_b_re[j], s5_b_im[j],
                          s5_c_re[j], s5_c_im[j], s5_d[j], s5_w_glu[j], s5_b_glu[j])
        else:
            x = _ssd_layer(x, norm_mix_g[i], sh1, sc1, g1, m2_w_in[j], m2_conv_w[j],
                           m2_conv_b[j], m2_dt_bias[j], m2_a_log[j], m2_d[j],
                           m2_norm_g[j], m2_w_out[j])
        x = _mlp_layer(x, norm_mlp_g[i], sh2, sc2, g2, mlp_w1[i], mlp_w2[i],
                       final_g=final_norm_g if i == depth - 1 else None)
    return x
```

```python
import functools

import jax
import jax.numpy as jnp
from jax import lax
from jax.experimental import pallas as pl
from jax.experimental.pallas import tpu as pltpu

F32 = jnp.float32
BF16 = jnp.bfloat16
HIGHEST = lax.Precision.HIGHEST
NORM_EPS = 1e-5
N_MOD = 6
LOG2_E = 1.4426950408889634

V7X_LANES = 128
V7X_SUBLANES = 8
V7X_SCOPED_VMEM_BYTES = 60000 * 1024

S5_GROUP = 16
S5_STATE = 64
S5_Q = 16
S5_QW = S5_Q * S5_GROUP
S5_PAIR_STATE = 4 * S5_STATE
S5_IN_CHUNKS = 256
S5_OUT_CHUNKS = 128
S5_OUT_OFFSETS = 4

M2_HEADDIM = 64
M2_GROUPS = 4
M2_STATE = 128
M2_CONV = 4
M2_CHUNK = 128
M2_HEAD_PAIR = 2 * M2_HEADDIM
M2_PHASES = V7X_SUBLANES
M2_PHASE_ROWS = M2_CHUNK // M2_PHASES
M2_BLOCK_CHUNKS = 4

TOKEN_BLOCK = 512
SCAN_BLOCK = 64


def _compiler_params(semantics, block_bytes, temp_bytes):
    want = 2 * block_bytes + temp_bytes
    return pltpu.CompilerParams(
        dimension_semantics=semantics,
        vmem_limit_bytes=int(min(V7X_SCOPED_VMEM_BYTES, max(want, 16 * 1024 * 1024))))


def _nbytes(shape, dtype):
    n = 1
    for s in shape:
        n *= s
    return n * jnp.dtype(dtype).itemsize


def _const_spec(shape):
    return pl.BlockSpec(shape, lambda *_: (0,) * len(shape), pipeline_mode=pl.Buffered(1))


def _rms_mod(x, g, shift, scale):
    y = x * lax.rsqrt(jnp.mean(x * x, axis=-1, keepdims=True) + NORM_EPS)
    return (y * g) * (1.0 + scale) + shift


def _softplus(x):
    return jnp.maximum(x, 0.0) + jnp.log1p(jnp.exp(-jnp.abs(x)))


def _ada_kernel(c_ref, w_ref, b_ref, o_ref):
    cond = jax.nn.silu(c_ref[...])
    o_ref[0] = jnp.dot(cond, w_ref[0], precision=HIGHEST,
                       preferred_element_type=F32) + b_ref[0]


def _ada_modulation(c, ada_w, ada_b):
    depth, d, n = ada_w.shape
    bsz = c.shape[0]
    rows = V7X_SUBLANES
    nb = n // 4
    c_pad = jnp.zeros((rows, d), F32).at[:bsz].set(c)
    out = pl.pallas_call(
        _ada_kernel,
        grid=(depth, n // nb),
        in_specs=[pl.BlockSpec((rows, d), lambda i, j: (0, 0)),
                  pl.BlockSpec((1, d, nb), lambda i, j: (i, 0, j)),
                  pl.BlockSpec((1, 1, nb), lambda i, j: (i, 0, j))],
        out_specs=pl.BlockSpec((1, rows, nb), lambda i, j: (i, 0, j)),
        out_shape=jax.ShapeDtypeStruct((depth, rows, n), F32),
        compiler_params=_compiler_params(("arbitrary", "arbitrary"),
                                         _nbytes((d, nb), F32), 4 * 1024 * 1024),
        name="ada_modulation",
    )(c_pad, ada_w, ada_b.reshape(depth, 1, n))
    return out[:, :bsz, :]


def _mlp_kernel(x_ref, g_ref, sh_ref, sc_ref, gate_ref, w1_ref, w2_ref, *rest, final):
    if final:
        fg_ref, o_ref = rest
    else:
        (o_ref,) = rest
    x = x_ref[0]
    h = _rms_mod(x, g_ref[...], sh_ref[0], sc_ref[0])
    a = jnp.maximum(jnp.dot(h.astype(BF16), w1_ref[...], preferred_element_type=F32), 0.0)
    y = jnp.dot((a * a).astype(BF16), w2_ref[...], preferred_element_type=F32)
    out = x + gate_ref[0] * y
    if final:
        out = out * lax.rsqrt(jnp.mean(out * out, axis=-1, keepdims=True) + NORM_EPS)
        out = out * fg_ref[...]
    o_ref[0] = out


def _mlp_layer(x, norm_g, shift, scale, gate, w1, w2, final_g=None):
    bsz, seq, d = x.shape
    f = w1.shape[1]
    tl = TOKEN_BLOCK
    tok = pl.BlockSpec((1, tl, d), lambda b, i: (b, i, 0))
    per_b = pl.BlockSpec((1, 1, d), lambda b, i: (b, 0, 0))
    in_specs = [tok, _const_spec((1, d)), per_b, per_b, per_b,
                _const_spec((d, f)), _const_spec((f, d))]
    args = [x, norm_g.reshape(1, d), shift, scale, gate, w1.astype(BF16), w2.astype(BF16)]
    if final_g is not None:
        in_specs.append(_const_spec((1, d)))
        args.append(final_g.reshape(1, d))
    return pl.pallas_call(
        functools.partial(_mlp_kernel, final=final_g is not None),
        grid=(bsz, seq // tl),
        in_specs=in_specs,
        out_specs=tok,
        out_shape=jax.ShapeDtypeStruct(x.shape, F32),
        compiler_params=_compiler_params(
            ("parallel", "parallel"),
            2 * _nbytes((tl, d), F32) + _nbytes((d, f), BF16),
            _nbytes((tl, f), F32) * 2 + _nbytes((tl, d), F32) * 2),
        name="mlp",
    )(*args)


def _stage_lane_tiles(x_ref, tiles):
    r, s, _ = x_ref.shape[1:]
    for j, t_ref in enumerate(tiles):
        t_ref[...] = x_ref[0, :, :, j * V7X_LANES:(j + 1) * V7X_LANES].reshape(r * s, V7X_LANES)


def _unstage_lane_tiles(tiles, o_ref):
    r, s, _ = o_ref.shape[1:]
    for j, t_ref in enumerate(tiles):
        o_ref[0, :, :, j * V7X_LANES:(j + 1) * V7X_LANES] = t_ref[...].reshape(r, s, V7X_LANES)


def _strided_rows(tiles, start, size, stride):
    return jnp.concatenate([t[pl.ds(start, size, stride=stride), :] for t in tiles], axis=1)


def _set_strided_rows(tiles, start, size, stride, value):
    for j, t_ref in enumerate(tiles):
        t_ref[pl.ds(start, size, stride=stride), :] = value[:, j * V7X_LANES:(j + 1) * V7X_LANES]


def _s5_in_kernel(x_ref, g_ref, sh_ref, sc_ref, w_ref, u_ref, *tiles):
    nc, tg = x_ref.shape[1:3]
    _stage_lane_tiles(x_ref, tiles)
    for j in range(tg):
        h = _rms_mod(_strided_rows(tiles, j, nc, tg), g_ref[...], sh_ref[0], sc_ref[0])
        u = jnp.dot(h.astype(BF16), w_ref[...], preferred_element_type=F32)
        u_ref[0, j] = u.T.astype(BF16)


def _group_inputs(x_ref, gi):
    q, _, nc = x_ref.shape[1:]
    return x_ref[0, :, gi * S5_GROUP:(gi + 1) * S5_GROUP, :].reshape(q * S5_GROUP, nc)


def _s5_state_kernel(x_ref, ws_ref, s_ref):
    p = S5_STATE
    parts = [jnp.dot(ws_ref[gi], _group_inputs(x_ref, gi), preferred_element_type=F32)
             for gi in range(2)]
    st = jnp.concatenate([parts[0][:p], parts[1][:p], parts[0][p:], parts[1][p:]], axis=0)
    s_ref[...] = st.T


def _s5_scan_kernel(a_ref, s_ref, o_ref, h_ref):
    half = a_ref.shape[1] // 2

    @pl.when(pl.program_id(0) == 0)
    def _():
        h_ref[...] = jnp.zeros_like(h_ref)

    a_re = a_ref[:, :half]
    a_im = a_ref[:, half:]

    def body(c, carry):
        h_re, h_im = carry
        o_ref[c, :, :half] = h_re
        o_ref[c, :, half:] = h_im
        s = s_ref[c]
        return (a_re * h_re - a_im * h_im + s[:, :half],
                a_re * h_im + a_im * h_re + s[:, half:])

    h_re, h_im = lax.fori_loop(0, s_ref.shape[0], body,
                               (h_ref[:, :half], h_ref[:, half:]))
    h_ref[:, :half] = h_re
    h_ref[:, half:] = h_im


def _s5_y_kernel(x_ref, h_ref, t_ref, wy_ref, y_ref):
    q = x_ref.shape[1]
    nc = x_ref.shape[3]
    p = S5_STATE
    ht = h_ref[...].T.astype(BF16)
    for gi in range(2):
        hg = jnp.concatenate([ht[gi * p:(gi + 1) * p], ht[(2 + gi) * p:(3 + gi) * p]], axis=0)
        y = jnp.dot(t_ref[gi], _group_inputs(x_ref, gi), preferred_element_type=F32)
        y += jnp.dot(wy_ref[gi], hg, preferred_element_type=F32)
        y_ref[0, :, gi * S5_GROUP:(gi + 1) * S5_GROUP, :] = y.reshape(q, S5_GROUP, nc)


def _s5_out_kernel(y_ref, x_ref, w_ref, b_ref, gate_ref, o_ref, *tiles):
    d = w_ref.shape[0]
    nt, nc = y_ref.shape[1], y_ref.shape[3]
    tg = S5_OUT_OFFSETS
    _stage_lane_tiles(x_ref, tiles)
    for j0 in range(0, nt, tg):
        g = jnp.concatenate([jax.nn.gelu(y_ref[0, j0 + j]).T for j in range(tg)], axis=0)
        ab = jnp.dot(g.astype(BF16), w_ref[...], preferred_element_type=F32) + b_ref[...]
        mix = ab[:, :d] * jax.nn.sigmoid(ab[:, d:])
        for j in range(tg):
            out = (_strided_rows(tiles, j0 + j, nc, nt)
                   + gate_ref[0] * mix[j * nc:(j + 1) * nc])
            _set_strided_rows(tiles, j0 + j, nc, nt, out)
    _unstage_lane_tiles(tiles, o_ref)


def _s5_tables(lam_re, lam_im, log_dt, b_re, b_im, c_re, c_im, d_skip):
    g, p = lam_re.shape
    q, hh = S5_Q, S5_GROUP
    dt = jnp.exp(log_dt)[:, None]
    ld_re, ld_im = lam_re * dt, lam_im * dt

    def lam_pow(k):
        mag = jnp.exp(ld_re[..., None] * k)
        ang = ld_im[..., None] * k
        return mag * jnp.cos(ang), mag * jnp.sin(ang)

    pw_re, pw_im = lam_pow(jnp.arange(q + 1, dtype=F32))
    lb_re, lb_im = pw_re[..., 1], pw_im[..., 1]
    den = lam_re * lam_re + lam_im * lam_im
    f_re = ((lb_re - 1.0) * lam_re + lb_im * lam_im) / den
    f_im = (lb_im * lam_re - (lb_re - 1.0) * lam_im) / den
    bb_re = f_re[..., None] * b_re - f_im[..., None] * b_im
    bb_im = f_re[..., None] * b_im + f_im[..., None] * b_re

    kp_re = jnp.transpose(pw_re[..., :q], (0, 2, 1))[:, :, None, :]
    kp_im = jnp.transpose(pw_im[..., :q], (0, 2, 1))[:, :, None, :]
    cp_re = c_re[:, None] * kp_re - c_im[:, None] * kp_im
    cp_im = c_re[:, None] * kp_im + c_im[:, None] * kp_re
    lag = (jnp.einsum('gkop,gpi->gkoi', cp_re, bb_re, precision=HIGHEST)
           - jnp.einsum('gkop,gpi->gkoi', cp_im, bb_im, precision=HIGHEST))
    lag = lag.at[:, 0].add(d_skip.reshape(g, hh)[:, :, None] * jnp.eye(hh, dtype=F32))
    tt = jnp.arange(q)
    sub_diag = (tt[None, :, None] - tt[None, None, :] == tt[:, None, None]).astype(F32)
    toep = jnp.einsum('kut,gkoh->guoth', sub_diag, lag, precision=HIGHEST).reshape(g, q * hh, q * hh)

    rv_re, rv_im = pw_re[..., :q][..., ::-1], pw_im[..., :q][..., ::-1]
    ws_re = rv_re[..., None] * bb_re[:, :, None, :] - rv_im[..., None] * bb_im[:, :, None, :]
    ws_im = rv_re[..., None] * bb_im[:, :, None, :] + rv_im[..., None] * bb_re[:, :, None, :]
    ws = jnp.stack([ws_re, ws_im], axis=1).reshape(g, 2 * p, q * hh)

    up_re = jnp.transpose(pw_re[..., 1:], (0, 2, 1))[:, :, None, :]
    up_im = jnp.transpose(pw_im[..., 1:], (0, 2, 1))[:, :, None, :]
    cy_re = c_re[:, None] * up_re - c_im[:, None] * up_im
    cy_im = c_re[:, None] * up_im + c_im[:, None] * up_re
    wy = jnp.stack([cy_re, -cy_im], axis=3).reshape(g, q * hh, 2 * p)

    a_p = jnp.concatenate([pw_re[..., q].reshape(g // 2, 2 * p),
                           pw_im[..., q].reshape(g // 2, 2 * p)], axis=1)
    return toep.astype(BF16), ws.astype(BF16), wy.astype(BF16), a_p


def _s5_layer(x, norm_g, shift, scale, gate, w_in, lam_re, lam_im, log_dt,
              b_re, b_im, c_re, c_im, d_skip, w_glu, b_glu):
    bsz, seq, d = x.shape
    groups = d // S5_GROUP
    pairs = groups // 2
    nc = seq // S5_Q
    pair_rows = 2 * S5_GROUP
    per_b = pl.BlockSpec((1, 1, d), lambda b, i, k: (b, 0, 0))

    xv = x.reshape(bsz, nc, S5_Q, d)
    tg = V7X_SUBLANES

    def tok_spec(ncb):
        return pl.BlockSpec((1, ncb, tg, d), lambda b, i, k: (b, i, k, 0))

    def chan_spec(ncb):
        return pl.BlockSpec((1, tg, d, ncb), lambda b, i, k: (b, k, 0, i))

    def lane_tiles(ncb):
        return [pltpu.VMEM((ncb * tg, V7X_LANES), F32)] * (d // V7X_LANES)

    ncb = S5_IN_CHUNKS
    ut = pl.pallas_call(
        _s5_in_kernel,
        grid=(bsz, nc // ncb, S5_Q // tg),
        in_specs=[tok_spec(ncb), _const_spec((1, d)), per_b, per_b, _const_spec((d, d))],
        out_specs=chan_spec(ncb),
        out_shape=jax.ShapeDtypeStruct((bsz, S5_Q, d, nc), BF16),
        scratch_shapes=lane_tiles(ncb),
        compiler_params=_compiler_params(("parallel", "parallel", "parallel"),
                                         _nbytes((ncb, tg, d), F32) + _nbytes((tg, d, ncb), BF16),
                                         _nbytes((d, d), BF16) + _nbytes((ncb, tg, d), F32)
                                         + 8 * _nbytes((ncb, d), F32)),
        name="s5_in",
    )(xv, norm_g.reshape(1, d), shift, scale, w_in.astype(BF16))

    toep, ws, wy, a_pair = _s5_tables(lam_re, lam_im, log_dt, b_re, b_im, c_re, c_im, d_skip)

    rows = bsz * pairs
    xspec = pl.BlockSpec((1, S5_Q, pair_rows, nc), lambda b, p: (b, 0, p, 0))
    sspec = pl.BlockSpec((nc, S5_PAIR_STATE), lambda b, p: (0, b * pairs + p))

    def table_spec(r, c):
        return pl.BlockSpec((2, r, c), lambda b, p: (p, 0, 0))

    s_loc = pl.pallas_call(
        _s5_state_kernel,
        grid=(bsz, pairs),
        in_specs=[xspec, table_spec(2 * S5_STATE, S5_QW)],
        out_specs=sspec,
        out_shape=jax.ShapeDtypeStruct((nc, rows * S5_PAIR_STATE), F32),
        compiler_params=_compiler_params(
            ("parallel", "parallel"),
            _nbytes((nc, 2 * S5_QW), BF16) + _nbytes((nc, S5_PAIR_STATE), F32),
            3 * _nbytes((nc, S5_PAIR_STATE), F32)),
        name="s5_chunk_state",
    )(ut, ws)

    cb = SCAN_BLOCK
    sblk = pl.BlockSpec((cb, rows, S5_PAIR_STATE), lambda i: (i, 0, 0))
    h_in = pl.pallas_call(
        _s5_scan_kernel,
        grid=(nc // cb,),
        in_specs=[_const_spec((rows, S5_PAIR_STATE)), sblk],
        out_specs=sblk,
        out_shape=jax.ShapeDtypeStruct((nc, rows, S5_PAIR_STATE), F32),
        scratch_shapes=[pltpu.VMEM((rows, S5_PAIR_STATE), F32)],
        compiler_params=_compiler_params(("arbitrary",),
                                         2 * _nbytes((cb, rows, S5_PAIR_STATE), F32),
                                         1024 * 1024),
        name="s5_chunk_scan",
    )(jnp.tile(a_pair, (bsz, 1)), s_loc.reshape(nc, rows, S5_PAIR_STATE))

    yt = pl.pallas_call(
        _s5_y_kernel,
        grid=(bsz, pairs),
        in_specs=[xspec, sspec, table_spec(S5_QW, S5_QW), table_spec(S5_QW, 2 * S5_STATE)],
        out_specs=xspec,
        out_shape=jax.ShapeDtypeStruct((bsz, S5_Q, d, nc), F32),
        compiler_params=_compiler_params(
            ("parallel", "parallel"),
            _nbytes((nc, 2 * S5_QW), BF16) + _nbytes((nc, 2 * S5_QW), F32)
            + _nbytes((nc, S5_PAIR_STATE), F32),
            3 * _nbytes((nc, 2 * S5_QW), F32)),
        name="s5_chunk_output",
    )(ut, h_in.reshape(nc, rows * S5_PAIR_STATE), toep, wy)

    ncb = S5_OUT_CHUNKS
    out = pl.pallas_call(
        _s5_out_kernel,
        grid=(bsz, nc // ncb, S5_Q // tg),
        in_specs=[chan_spec(ncb), tok_spec(ncb), _const_spec((d, 2 * d)),
                  _const_spec((1, 2 * d)), per_b],
        out_specs=tok_spec(ncb),
        out_shape=jax.ShapeDtypeStruct(xv.shape, F32),
        scratch_shapes=lane_tiles(ncb),
        compiler_params=_compiler_params(
            ("parallel", "parallel", "parallel"),
            3 * _nbytes((ncb, tg, d), F32),
            _nbytes((d, 2 * d), BF16) + _nbytes((ncb, tg, d), F32)
            + 6 * _nbytes((S5_OUT_OFFSETS * ncb, 2 * d), F32)),
        name="s5_out",
    )(yt, xv, w_glu.astype(BF16), b_glu.reshape(1, 2 * d), gate)
    return out.reshape(bsz, seq, d)


def _ssd_kernel(x_ref, g_ref, sh_ref, sc_ref, gate_ref, wz_ref, wx_ref, wdtt_ref, cw_ref, cb_ref,
                biast_ref, alogt_ref, dskip_ref, ng_ref, wo_ref, o_ref,
                h_ref, xbc_ref, dtt_ref, y_ref, halo_ref, st_ref, *tiles, d_inner, heads):
    q = M2_CHUNK
    n = M2_STATE
    hp = M2_HEAD_PAIR
    ph_n = M2_PHASES
    rpc = M2_PHASE_ROWS
    kc = M2_BLOCK_CHUNKS
    halo = V7X_SUBLANES
    n_prev = M2_CONV - 1

    @pl.when(pl.program_id(1) == 0)
    def _():
        halo_ref[:, halo - 1:halo, :] = jnp.zeros((n_prev, 1, halo_ref.shape[2]), F32)
        st_ref[...] = jnp.zeros_like(st_ref)

    _stage_lane_tiles(x_ref, tiles)
    xb = jnp.concatenate([_strided_rows(tiles, k * q + ph, rpc, ph_n)
                          for k in range(kc) for ph in range(ph_n)], axis=0)
    h = _rms_mod(xb, g_ref[...], sh_ref[0], sc_ref[0]).astype(BF16)
    h_ref[...] = h
    xr = jnp.dot(h, wx_ref[...], preferred_element_type=F32)
    dtt = _softplus(lax.dot_general(wdtt_ref[...], h, (((1,), (1,)), ((), ())),
                                    preferred_element_type=F32) + biast_ref[...])
    for k in range(kc):
        dtt_ref[k] = dtt[:, k * q:(k + 1) * q]

    for j in range(n_prev):
        ph = ph_n - n_prev + j
        for k in range(kc):
            halo_ref[j, halo + k * rpc:halo + (k + 1) * rpc, :] = (
                xr[k * q + ph * rpc:k * q + (ph + 1) * rpc, :])
    prev = [halo_ref[j, pl.ds(halo - 1, kc * rpc), :] for j in range(n_prev)]
    for j in range(n_prev):
        halo_ref[j, halo - 1:halo, :] = halo_ref[j, halo + kc * rpc - 1:halo + kc * rpc, :]
    for k in range(kc):
        xk = xr[k * q:(k + 1) * q, :]
        pk = jnp.concatenate([prev[j][k * rpc:(k + 1) * rpc] for j in range(n_prev)], axis=0)
        acc = cb_ref[...] + cw_ref[M2_CONV - 1:M2_CONV, :] * xk
        for s in range(1, M2_CONV):
            shifted = jnp.concatenate([pk[(n_prev - s) * rpc:], xk[:q - s * rpc]], axis=0)
            acc += cw_ref[M2_CONV - 1 - s:M2_CONV - s, :] * shifted
        xbc_ref[k] = acc * jax.nn.sigmoid(acc)

    shift_bits = rpc.bit_length() - 1
    row = lax.broadcasted_iota(jnp.int32, (q, q), 0)
    col = lax.broadcasted_iota(jnp.int32, (q, q), 1)
    pos_r = (row & (rpc - 1)) * ph_n + lax.shift_right_logical(row, shift_bits)
    pos_c = (col & (rpc - 1)) * ph_n + lax.shift_right_logical(col, shift_bits)
    causal = pos_r >= pos_c
    cum_mat = (pos_r <= pos_c).astype(F32)
    neg_a = -jnp.exp(alogt_ref[...])
    lane = lax.broadcasted_iota(jnp.int32, (q, hp), 1)
    first = lane < M2_HEADDIM
    first_row = lax.broadcasted_iota(jnp.int32, (1, hp), 1) < M2_HEADDIM
    heads_per_group = heads // M2_GROUPS
    pairs_per_group = heads_per_group // 2

    def chunk(k, carry):
        xbc = xbc_ref[k]
        dtk = dtt_ref[k]
        a_cst = LOG2_E * jnp.dot(dtk * neg_a, cum_mat, precision=HIGHEST,
                                 preferred_element_type=F32)
        src = a_cst - LOG2_E * jnp.log(dtk)
        a_cs = jnp.concatenate([a_cst, jnp.zeros((q - heads, q), F32)], axis=0).T
        state_w = jnp.exp2(a_cst[:, q - 1:q] - a_cst) * dtk
        chunk_decay = jnp.exp2(a_cs[q - 1:q, :])
        for g in range(M2_GROUPS):
            b_g = xbc[:, d_inner + g * n:d_inner + (g + 1) * n]
            c_g = xbc[:, d_inner + M2_GROUPS * n + g * n:d_inner + M2_GROUPS * n + (g + 1) * n]
            cb = lax.dot_general(c_g.astype(BF16), b_g.astype(BF16), (((1,), (1,)), ((), ())),
                                 preferred_element_type=F32)
            b_t = b_g.T
            for j in range(pairs_per_group):
                h0 = g * heads_per_group + 2 * j
                sl = slice((h0 // 2) * hp, (h0 // 2 + 1) * hp)
                lhs, lhs_state = [], []
                for hd in (h0, h0 + 1):
                    bc = jnp.broadcast_to(a_cs[:, hd:hd + 1], (q, q))
                    decay_dt = jnp.exp2(jnp.where(causal, bc - src[hd:hd + 1, :], -jnp.inf))
                    lhs.append(cb * decay_dt)
                    lhs_state.append(b_t * state_w[hd:hd + 1, :])
                for hd in (h0, h0 + 1):
                    lhs.append(c_g * jnp.exp2(jnp.broadcast_to(a_cs[:, hd:hd + 1], (q, n))))
                xs_p = xbc[:, sl]
                st_p = st_ref[:, sl]
                x0 = jnp.where(first, xs_p, 0.0)
                x1 = jnp.where(first, 0.0, xs_p)
                s0 = jnp.where(first, st_p, 0.0)
                s1 = jnp.where(first, 0.0, st_p)
                lhs = jnp.concatenate(lhs, axis=1).astype(BF16)
                rhs = jnp.concatenate([x0, x1, s0, s1], axis=0).astype(BF16)
                y = jnp.dot(lhs, rhs, preferred_element_type=F32)
                y_ref[k, :, sl] = y + xs_p * dskip_ref[:, sl]
                s_new = jnp.dot(jnp.concatenate(lhs_state, axis=1).astype(BF16),
                                jnp.concatenate([x0, x1], axis=0).astype(BF16),
                                preferred_element_type=F32)
                cd = jnp.where(first_row, chunk_decay[:, h0:h0 + 1], chunk_decay[:, h0 + 1:h0 + 2])
                st_ref[:, sl] = st_p * cd + s_new
        return carry

    lax.fori_loop(0, kc, chunk, 0)

    z = jnp.dot(h_ref[...], wz_ref[...], preferred_element_type=F32)
    y = y_ref[...].reshape(kc * q, d_inner) * (z * jax.nn.sigmoid(z))
    gw = d_inner // M2_GROUPS
    parts = []
    for g in range(M2_GROUPS):
        yg = y[:, g * gw:(g + 1) * gw]
        parts.append(yg * lax.rsqrt(jnp.mean(yg * yg, axis=-1, keepdims=True) + NORM_EPS))
    y = jnp.concatenate(parts, axis=1) * ng_ref[...]
    out = jnp.dot(y.astype(BF16), wo_ref[...], preferred_element_type=F32)
    for k in range(kc):
        for ph in range(ph_n):
            r0 = k * q + ph * rpc
            res = (_strided_rows(tiles, k * q + ph, rpc, ph_n) + gate_ref[0] * out[r0:r0 + rpc])
            _set_strided_rows(tiles, k * q + ph, rpc, ph_n, res)
    _unstage_lane_tiles(tiles, o_ref)


def _ssd_layer(x, norm_g, shift, scale, gate, w_in, conv_w, conv_b, dt_bias, a_log,
               d_skip, gn_g, w_out):
    bsz, seq, d = x.shape
    d_inner = w_out.shape[0]
    heads = d_inner // M2_HEADDIM
    conv_dim = d_inner + 2 * M2_GROUPS * M2_STATE
    q = M2_CHUNK
    kc = M2_BLOCK_CHUNKS
    rows = kc * M2_PHASE_ROWS

    w_z = w_in[:, :d_inner].astype(BF16)
    w_x = w_in[:, d_inner:d_inner + conv_dim].astype(BF16)
    w_dt_t = w_in[:, d_inner + conv_dim:].T.astype(BF16)

    xv = x.reshape(bsz, seq // M2_PHASES, M2_PHASES, d)
    blk = pl.BlockSpec((1, rows, M2_PHASES, d), lambda b, i: (b, i, 0, 0))
    per_b = pl.BlockSpec((1, 1, d), lambda b, i: (b, 0, 0))
    out = pl.pallas_call(
        functools.partial(_ssd_kernel, d_inner=d_inner, heads=heads),
        grid=(bsz, seq // (kc * q)),
        in_specs=[blk, _const_spec((1, d)), per_b, per_b, per_b,
                  _const_spec((d, d_inner)), _const_spec((d, conv_dim)), _const_spec((heads, d)),
                  _const_spec((M2_CONV, conv_dim)), _const_spec((1, conv_dim)),
                  _const_spec((heads, 1)), _const_spec((heads, 1)),
                  _const_spec((1, d_inner)), _const_spec((1, d_inner)), _const_spec((d_inner, d))],
        out_specs=blk,
        out_shape=jax.ShapeDtypeStruct(xv.shape, F32),
        scratch_shapes=[pltpu.VMEM((kc * q, d), BF16),
                        pltpu.VMEM((kc, q, conv_dim), F32),
                        pltpu.VMEM((kc, heads, q), F32),
                        pltpu.VMEM((kc, q, d_inner), F32),
                        pltpu.VMEM((M2_CONV - 1, V7X_SUBLANES + rows, conv_dim), F32),
                        pltpu.VMEM((M2_STATE, d_inner), F32)]
        + [pltpu.VMEM((kc * q, V7X_LANES), F32)] * (d // V7X_LANES),
        compiler_params=_compiler_params(
            ("arbitrary", "arbitrary"),
            2 * _nbytes((kc * q, d), F32),
            _nbytes((d, 2 * d_inner + conv_dim), BF16)
            + 2 * _nbytes((kc * q, conv_dim + d_inner), F32)
            + 2 * _nbytes((kc * q, conv_dim), F32)),
        name="ssd_mixer",
    )(xv, norm_g.reshape(1, d), shift, scale, gate, w_z, w_x, w_dt_t,
      conv_w, conv_b.reshape(1, conv_dim), dt_bias.reshape(heads, 1), a_log.reshape(heads, 1),
      jnp.repeat(d_skip, M2_HEADDIM).reshape(1, d_inner), gn_g.reshape(1, d_inner),
      w_out.astype(BF16))
    return out.reshape(bsz, seq, d)


def kernel(x, c, ada_w, ada_b, norm_mix_g, norm_mlp_g, mlp_w1, mlp_w2, s5_w_in, s5_lambda_re, s5_lambda_im, s5_log_dt, s5_b_re, s5_b_im, s5_c_re, s5_c_im, s5_d, s5_w_glu, s5_b_glu, m2_w_in, m2_conv_w, m2_conv_b, m2_dt_bias, m2_a_log, m2_d, m2_norm_g, m2_w_out, final_norm_g):
    depth = ada_w.shape[0]
    bsz, _, d = x.shape
    mod = _ada_modulation(c, ada_w, ada_b).reshape(depth, bsz, N_MOD, 1, d)
    for i in range(depth):
        sh1, sc1, g1, sh2, sc2, g2 = (mod[i, :, k] for k in range(N_MOD))
        j = i // 2
        if i % 2 == 0:
            x = _s5_layer(x, norm_mix_g[i], sh1, sc1, g1, s5_w_in[j], s5_lambda_re[j],
                          s5_lambda_im[j], s5_log_dt[j], s5_b_re[j], s5_b_im[j],
                          s5_c_re[j], s5_c_im[j], s5_d[j], s5_w_glu[j], s5_b_glu[j])
        else:
            x = _ssd_layer(x, norm_mix_g[i], sh1, sc1, g1, m2_w_in[j], m2_conv_w[j],
                           m2_conv_b[j], m2_dt_bias[j], m2_a_log[j], m2_d[j],
                           m2_norm_g[j], m2_w_out[j])
        x = _mlp_layer(x, norm_mlp_g[i], sh2, sc2, g2, mlp_w1[i], mlp_w2[i],
                       final_g=final_norm_g if i == depth - 1 else None)
    return x
```

```python
import functools

import jax
import jax.numpy as jnp
from jax import lax
from jax.experimental import pallas as pl
from jax.experimental.pallas import tpu as pltpu

F32 = jnp.float32
BF16 = jnp.bfloat16
HIGHEST = lax.Precision.HIGHEST
NORM_EPS = 1e-5
N_MOD = 6
LOG2_E = 1.4426950408889634

V7X_LANES = 128
V7X_SUBLANES = 8
V7X_SCOPED_VMEM_BYTES = 60000 * 1024

S5_GROUP = 16
S5_STATE = 64
S5_Q = 16
S5_QW = S5_Q * S5_GROUP
S5_PAIR_STATE = 4 * S5_STATE
S5_IN_CHUNKS = 256
S5_OUT_CHUNKS = 128
S5_OUT_OFFSETS = 4

M2_HEADDIM = 64
M2_GROUPS = 4
M2_STATE = 128
M2_CONV = 4
M2_CHUNK = 128
M2_HEAD_PAIR = 2 * M2_HEADDIM
M2_PHASES = V7X_SUBLANES
M2_PHASE_ROWS = M2_CHUNK // M2_PHASES
M2_BLOCK_CHUNKS = 4

TOKEN_BLOCK = 512
SCAN_BLOCK = 64


def _compiler_params(semantics, block_bytes, temp_bytes):
    want = 2 * block_bytes + temp_bytes
    return pltpu.CompilerParams(
        dimension_semantics=semantics,
        vmem_limit_bytes=int(min(V7X_SCOPED_VMEM_BYTES, max(want, 16 * 1024 * 1024))))


def _nbytes(shape, dtype):
    n = 1
    for s in shape:
        n *= s
    return n * jnp.dtype(dtype).itemsize


def _const_spec(shape):
    return pl.BlockSpec(shape, lambda *_: (0,) * len(shape), pipeline_mode=pl.Buffered(1))


def _rms_mod(x, g, shift, scale):
    y = x * lax.rsqrt(jnp.mean(x * x, axis=-1, keepdims=True) + NORM_EPS)
    return (y * g) * (1.0 + scale) + shift


def _softplus(x):
    return jnp.maximum(x, 0.0) + jnp.log1p(jnp.exp(-jnp.abs(x)))


def _sigmoid(x):
    return 0.5 * jnp.tanh(0.5 * x) + 0.5


def _ada_kernel(c_ref, w_ref, b_ref, o_ref):
    cond = jax.nn.silu(c_ref[...])
    o_ref[0] = jnp.dot(cond, w_ref[0], precision=HIGHEST,
                       preferred_element_type=F32) + b_ref[0]


def _ada_modulation(c, ada_w, ada_b):
    depth, d, n = ada_w.shape
    bsz = c.shape[0]
    rows = V7X_SUBLANES
    nb = n // 4
    c_pad = jnp.zeros((rows, d), F32).at[:bsz].set(c)
    out = pl.pallas_call(
        _ada_kernel,
        grid=(depth, n // nb),
        in_specs=[pl.BlockSpec((rows, d), lambda i, j: (0, 0)),
                  pl.BlockSpec((1, d, nb), lambda i, j: (i, 0, j)),
                  pl.BlockSpec((1, 1, nb), lambda i, j: (i, 0, j))],
        out_specs=pl.BlockSpec((1, rows, nb), lambda i, j: (i, 0, j)),
        out_shape=jax.ShapeDtypeStruct((depth, rows, n), F32),
        compiler_params=_compiler_params(("arbitrary", "arbitrary"),
                                         _nbytes((d, nb), F32), 4 * 1024 * 1024),
        name="ada_modulation",
    )(c_pad, ada_w, ada_b.reshape(depth, 1, n))
    return out[:, :bsz, :]


def _mlp_kernel(x_ref, g_ref, sh_ref, sc_ref, gate_ref, w1_ref, w2_ref, *rest, final):
    if final:
        fg_ref, o_ref = rest
    else:
        (o_ref,) = rest
    x = x_ref[0]
    h = _rms_mod(x, g_ref[...], sh_ref[0], sc_ref[0])
    a = jnp.maximum(jnp.dot(h.astype(BF16), w1_ref[...], preferred_element_type=F32), 0.0)
    y = jnp.dot((a * a).astype(BF16), w2_ref[...], preferred_element_type=F32)
    out = x + gate_ref[0] * y
    if final:
        out = out * lax.rsqrt(jnp.mean(out * out, axis=-1, keepdims=True) + NORM_EPS)
        out = out * fg_ref[...]
    o_ref[0] = out


def _mlp_layer(x, norm_g, shift, scale, gate, w1, w2, final_g=None):
    bsz, seq, d = x.shape
    f = w1.shape[1]
    tl = TOKEN_BLOCK
    tok = pl.BlockSpec((1, tl, d), lambda b, i: (b, i, 0))
    per_b = pl.BlockSpec((1, 1, d), lambda b, i: (b, 0, 0))
    in_specs = [tok, _const_spec((1, d)), per_b, per_b, per_b,
                _const_spec((d, f)), _const_spec((f, d))]
    args = [x, norm_g.reshape(1, d), shift, scale, gate, w1.astype(BF16), w2.astype(BF16)]
    if final_g is not None:
        in_specs.append(_const_spec((1, d)))
        args.append(final_g.reshape(1, d))
    return pl.pallas_call(
        functools.partial(_mlp_kernel, final=final_g is not None),
        grid=(bsz, seq // tl),
        in_specs=in_specs,
        out_specs=tok,
        out_shape=jax.ShapeDtypeStruct(x.shape, F32),
        compiler_params=_compiler_params(
            ("parallel", "parallel"),
            2 * _nbytes((tl, d), F32) + _nbytes((d, f), BF16),
            _nbytes((tl, f), F32) * 2 + _nbytes((tl, d), F32) * 2),
        name="mlp",
    )(*args)


def _stage_lane_tiles(x_ref, tiles):
    r, s, _ = x_ref.shape[1:]
    for j, t_ref in enumerate(tiles):
        t_ref[...] = x_ref[0, :, :, j * V7X_LANES:(j + 1) * V7X_LANES].reshape(r * s, V7X_LANES)


def _unstage_lane_tiles(tiles, o_ref):
    r, s, _ = o_ref.shape[1:]
    for j, t_ref in enumerate(tiles):
        o_ref[0, :, :, j * V7X_LANES:(j + 1) * V7X_LANES] = t_ref[...].reshape(r, s, V7X_LANES)


def _strided_rows(tiles, start, size, stride):
    return jnp.concatenate([t[pl.ds(start, size, stride=stride), :] for t in tiles], axis=1)


def _set_strided_rows(tiles, start, size, stride, value):
    for j, t_ref in enumerate(tiles):
        t_ref[pl.ds(start, size, stride=stride), :] = value[:, j * V7X_LANES:(j + 1) * V7X_LANES]


def _s5_in_kernel(x_ref, g_ref, sh_ref, sc_ref, w_ref, u_ref, *tiles):
    nc, tg = x_ref.shape[1:3]
    _stage_lane_tiles(x_ref, tiles)
    for j in range(tg):
        h = _rms_mod(_strided_rows(tiles, j, nc, tg), g_ref[...], sh_ref[0], sc_ref[0])
        u = jnp.dot(h.astype(BF16), w_ref[...], preferred_element_type=F32)
        u_ref[0, j] = u.T.astype(BF16)


def _group_inputs(x_ref, gi):
    q, _, nc = x_ref.shape[1:]
    return x_ref[0, :, gi * S5_GROUP:(gi + 1) * S5_GROUP, :].reshape(q * S5_GROUP, nc)


def _s5_state_kernel(x_ref, ws_ref, s_ref):
    p = S5_STATE
    parts = [jnp.dot(ws_ref[gi], _group_inputs(x_ref, gi), preferred_element_type=F32)
             for gi in range(2)]
    st = jnp.concatenate([parts[0][:p], parts[1][:p], parts[0][p:], parts[1][p:]], axis=0)
    s_ref[...] = st.T


def _s5_scan_kernel(a_ref, s_ref, o_ref, h_ref):
    half = a_ref.shape[1] // 2

    @pl.when(pl.program_id(0) == 0)
    def _():
        h_ref[...] = jnp.zeros_like(h_ref)

    a_re = a_ref[:, :half]
    a_im = a_ref[:, half:]

    def body(c, carry):
        h_re, h_im = carry
        o_ref[c, :, :half] = h_re
        o_ref[c, :, half:] = h_im
        s = s_ref[c]
        return (a_re * h_re - a_im * h_im + s[:, :half],
                a_re * h_im + a_im * h_re + s[:, half:])

    h_re, h_im = lax.fori_loop(0, s_ref.shape[0], body,
                               (h_ref[:, :half], h_ref[:, half:]))
    h_ref[:, :half] = h_re
    h_ref[:, half:] = h_im


def _s5_y_kernel(x_ref, h_ref, t_ref, wy_ref, y_ref):
    q = x_ref.shape[1]
    nc = x_ref.shape[3]
    p = S5_STATE
    ht = h_ref[...].T.astype(BF16)
    for gi in range(2):
        hg = jnp.concatenate([ht[gi * p:(gi + 1) * p], ht[(2 + gi) * p:(3 + gi) * p]], axis=0)
        y = jnp.dot(t_ref[gi], _group_inputs(x_ref, gi), preferred_element_type=F32)
        y += jnp.dot(wy_ref[gi], hg, preferred_element_type=F32)
        y_ref[0, :, gi * S5_GROUP:(gi + 1) * S5_GROUP, :] = y.reshape(q, S5_GROUP, nc)


def _s5_out_kernel(y_ref, x_ref, w_ref, b_ref, gate_ref, o_ref, *tiles):
    d = w_ref.shape[0]
    nt, nc = y_ref.shape[1], y_ref.shape[3]
    tg = S5_OUT_OFFSETS
    _stage_lane_tiles(x_ref, tiles)
    for j0 in range(0, nt, tg):
        g = jnp.concatenate([jax.nn.gelu(y_ref[0, j0 + j]).T for j in range(tg)], axis=0)
        ab = jnp.dot(g.astype(BF16), w_ref[...], preferred_element_type=F32) + b_ref[...]
        mix = ab[:, :d] * _sigmoid(ab[:, d:])
        for j in range(tg):
            out = (_strided_rows(tiles, j0 + j, nc, nt)
                   + gate_ref[0] * mix[j * nc:(j + 1) * nc])
            _set_strided_rows(tiles, j0 + j, nc, nt, out)
    _unstage_lane_tiles(tiles, o_ref)


def _s5_tables(lam_re, lam_im, log_dt, b_re, b_im, c_re, c_im, d_skip):
    g, p = lam_re.shape
    q, hh = S5_Q, S5_GROUP
    dt = jnp.exp(log_dt)[:, None]
    ld_re, ld_im = lam_re * dt, lam_im * dt

    def lam_pow(k):
        mag = jnp.exp(ld_re[..., None] * k)
        ang = ld_im[..., None] * k
        return mag * jnp.cos(ang), mag * jnp.sin(ang)

    pw_re, pw_im = lam_pow(jnp.arange(q + 1, dtype=F32))
    lb_re, lb_im = pw_re[..., 1], pw_im[..., 1]
    den = lam_re * lam_re + lam_im * lam_im
    f_re = ((lb_re - 1.0) * lam_re + lb_im * lam_im) / den
    f_im = (lb_im * lam_re - (lb_re - 1.0) * lam_im) / den
    bb_re = f_re[..., None] * b_re - f_im[..., None] * b_im
    bb_im = f_re[..., None] * b_im + f_im[..., None] * b_re

    kp_re = jnp.transpose(pw_re[..., :q], (0, 2, 1))[:, :, None, :]
    kp_im = jnp.transpose(pw_im[..., :q], (0, 2, 1))[:, :, None, :]
    cp_re = c_re[:, None] * kp_re - c_im[:, None] * kp_im
    cp_im = c_re[:, None] * kp_im + c_im[:, None] * kp_re
    lag = (jnp.einsum('gkop,gpi->gkoi', cp_re, bb_re, precision=HIGHEST)
           - jnp.einsum('gkop,gpi->gkoi', cp_im, bb_im, precision=HIGHEST))
    lag = lag.at[:, 0].add(d_skip.reshape(g, hh)[:, :, None] * jnp.eye(hh, dtype=F32))
    tt = jnp.arange(q)
    sub_diag = (tt[None, :, None] - tt[None, None, :] == tt[:, None, None]).astype(F32)
    toep = jnp.einsum('kut,gkoh->guoth', sub_diag, lag, precision=HIGHEST).reshape(g, q * hh, q * hh)

    rv_re, rv_im = pw_re[..., :q][..., ::-1], pw_im[..., :q][..., ::-1]
    ws_re = rv_re[..., None] * bb_re[:, :, None, :] - rv_im[..., None] * bb_im[:, :, None, :]
    ws_im = rv_re[..., None] * bb_im[:, :, None, :] + rv_im[..., None] * bb_re[:, :, None, :]
    ws = jnp.stack([ws_re, ws_im], axis=1).reshape(g, 2 * p, q * hh)

    up_re = jnp.transpose(pw_re[..., 1:], (0, 2, 1))[:, :, None, :]
    up_im = jnp.transpose(pw_im[..., 1:], (0, 2, 1))[:, :, None, :]
    cy_re = c_re[:, None] * up_re - c_im[:, None] * up_im
    cy_im = c_re[:, None] * up_im + c_im[:, None] * up_re
    wy = jnp.stack([cy_re, -cy_im], axis=3).reshape(g, q * hh, 2 * p)

    a_p = jnp.concatenate([pw_re[..., q].reshape(g // 2, 2 * p),
                           pw_im[..., q].reshape(g // 2, 2 * p)], axis=1)
    return toep.astype(BF16), ws.astype(BF16), wy.astype(BF16), a_p


def _s5_layer(x, norm_g, shift, scale, gate, w_in, lam_re, lam_im, log_dt,
              b_re, b_im, c_re, c_im, d_skip, w_glu, b_glu):
    bsz, seq, d = x.shape
    groups = d // S5_GROUP
    pairs = groups // 2
    nc = seq // S5_Q
    pair_rows = 2 * S5_GROUP
    per_b = pl.BlockSpec((1, 1, d), lambda b, i, k: (b, 0, 0))

    xv = x.reshape(bsz, nc, S5_Q, d)
    tg = V7X_SUBLANES

    def tok_spec(ncb):
        return pl.BlockSpec((1, ncb, tg, d), lambda b, i, k: (b, i, k, 0))

    def chan_spec(ncb):
        return pl.BlockSpec((1, tg, d, ncb), lambda b, i, k: (b, k, 0, i))

    def lane_tiles(ncb):
        return [pltpu.VMEM((ncb * tg, V7X_LANES), F32)] * (d // V7X_LANES)

    ncb = S5_IN_CHUNKS
    ut = pl.pallas_call(
        _s5_in_kernel,
        grid=(bsz, nc // ncb, S5_Q // tg),
        in_specs=[tok_spec(ncb), _const_spec((1, d)), per_b, per_b, _const_spec((d, d))],
        out_specs=chan_spec(ncb),
        out_shape=jax.ShapeDtypeStruct((bsz, S5_Q, d, nc), BF16),
        scratch_shapes=lane_tiles(ncb),
        compiler_params=_compiler_params(("parallel", "parallel", "parallel"),
                                         _nbytes((ncb, tg, d), F32) + _nbytes((tg, d, ncb), BF16),
                                         _nbytes((d, d), BF16) + _nbytes((ncb, tg, d), F32)
                                         + 8 * _nbytes((ncb, d), F32)),
        name="s5_in",
    )(xv, norm_g.reshape(1, d), shift, scale, w_in.astype(BF16))

    toep, ws, wy, a_pair = _s5_tables(lam_re, lam_im, log_dt, b_re, b_im, c_re, c_im, d_skip)

    rows = bsz * pairs
    xspec = pl.BlockSpec((1, S5_Q, pair_rows, nc), lambda b, p: (b, 0, p, 0))
    sspec = pl.BlockSpec((nc, S5_PAIR_STATE), lambda b, p: (0, b * pairs + p))

    def table_spec(r, c):
        return pl.BlockSpec((2, r, c), lambda b, p: (p, 0, 0))

    s_loc = pl.pallas_call(
        _s5_state_kernel,
        grid=(bsz, pairs),
        in_specs=[xspec, table_spec(2 * S5_STATE, S5_QW)],
        out_specs=sspec,
        out_shape=jax.ShapeDtypeStruct((nc, rows * S5_PAIR_STATE), F32),
        compiler_params=_compiler_params(
            ("parallel", "parallel"),
            _nbytes((nc, 2 * S5_QW), BF16) + _nbytes((nc, S5_PAIR_STATE), F32),
            3 * _nbytes((nc, S5_PAIR_STATE), F32)),
        name="s5_chunk_state",
    )(ut, ws)

    cb = SCAN_BLOCK
    sblk = pl.BlockSpec((cb, rows, S5_PAIR_STATE), lambda i: (i, 0, 0))
    h_in = pl.pallas_call(
        _s5_scan_kernel,
        grid=(nc // cb,),
        in_specs=[_const_spec((rows, S5_PAIR_STATE)), sblk],
        out_specs=sblk,
        out_shape=jax.ShapeDtypeStruct((nc, rows, S5_PAIR_STATE), F32),
        scratch_shapes=[pltpu.VMEM((rows, S5_PAIR_STATE), F32)],
        compiler_params=_compiler_params(("arbitrary",),
                                         2 * _nbytes((cb, rows, S5_PAIR_STATE), F32),
                                         1024 * 1024),
        name="s5_chunk_scan",
    )(jnp.tile(a_pair, (bsz, 1)), s_loc.reshape(nc, rows, S5_PAIR_STATE))

    yt = pl.pallas_call(
        _s5_y_kernel,
        grid=(bsz, pairs),
        in_specs=[xspec, sspec, table_spec(S5_QW, S5_QW), table_spec(S5_QW, 2 * S5_STATE)],
        out_specs=xspec,
        out_shape=jax.ShapeDtypeStruct((bsz, S5_Q, d, nc), F32),
        compiler_params=_compiler_params(
            ("parallel", "parallel"),
            _nbytes((nc, 2 * S5_QW), BF16) + _nbytes((nc, 2 * S5_QW), F32)
            + _nbytes((nc, S5_PAIR_STATE), F32),
            3 * _nbytes((nc, 2 * S5_QW), F32)),
        name="s5_chunk_output",
    )(ut, h_in.reshape(nc, rows * S5_PAIR_STATE), toep, wy)

    ncb = S5_OUT_CHUNKS
    out = pl.pallas_call(
        _s5_out_kernel,
        grid=(bsz, nc // ncb, S5_Q // tg),
        in_specs=[chan_spec(ncb), tok_spec(ncb), _const_spec((d, 2 * d)),
                  _const_spec((1, 2 * d)), per_b],
        out_specs=tok_spec(ncb),
        out_shape=jax.ShapeDtypeStruct(xv.shape, F32),
        scratch_shapes=lane_tiles(ncb),
        compiler_params=_compiler_params(
            ("parallel", "parallel", "parallel"),
            3 * _nbytes((ncb, tg, d), F32),
            _nbytes((d, 2 * d), BF16) + _nbytes((ncb, tg, d), F32)
            + 6 * _nbytes((S5_OUT_OFFSETS * ncb, 2 * d), F32)),
        name="s5_out",
    )(yt, xv, w_glu.astype(BF16), b_glu.reshape(1, 2 * d), gate)
    return out.reshape(bsz, seq, d)


def _ssd_kernel(x_ref, g_ref, sh_ref, sc_ref, gate_ref, wz_ref, wx_ref, wdtt_ref, cw_ref, cb_ref,
                biast_ref, alogt_ref, dskip_ref, ng_ref, wo_ref, o_ref,
                h_ref, xr_ref, dtt_ref, y_ref, halo_ref, prev_ref, st_ref, *tiles, d_inner, heads):
    q = M2_CHUNK
    n = M2_STATE
    hp = M2_HEAD_PAIR
    ph_n = M2_PHASES
    rpc = M2_PHASE_ROWS
    kc = M2_BLOCK_CHUNKS
    halo = V7X_SUBLANES
    n_prev = M2_CONV - 1

    @pl.when(pl.program_id(1) == 0)
    def _():
        halo_ref[:, halo - 1:halo, :] = jnp.zeros((n_prev, 1, halo_ref.shape[2]), F32)
        st_ref[...] = jnp.zeros_like(st_ref)

    _stage_lane_tiles(x_ref, tiles)
    xb = jnp.concatenate([_strided_rows(tiles, k * q + ph, rpc, ph_n)
                          for k in range(kc) for ph in range(ph_n)], axis=0)
    h = _rms_mod(xb, g_ref[...], sh_ref[0], sc_ref[0]).astype(BF16)
    h_ref[...] = h
    xr = jnp.dot(h, wx_ref[...], preferred_element_type=F32)
    dtt = _softplus(lax.dot_general(wdtt_ref[...], h, (((1,), (1,)), ((), ())),
                                    preferred_element_type=F32) + biast_ref[...])
    for k in range(kc):
        dtt_ref[k] = dtt[:, k * q:(k + 1) * q]

    for k in range(kc):
        xr_ref[k] = xr[k * q:(k + 1) * q, :]
    for j in range(n_prev):
        ph = ph_n - n_prev + j
        for k in range(kc):
            halo_ref[j, halo + k * rpc:halo + (k + 1) * rpc, :] = (
                xr[k * q + ph * rpc:k * q + (ph + 1) * rpc, :])
    for j in range(n_prev):
        back = halo_ref[j, pl.ds(halo - 1, kc * rpc), :]
        for k in range(kc):
            prev_ref[k, j * rpc:(j + 1) * rpc, :] = back[k * rpc:(k + 1) * rpc]
        halo_ref[j, halo - 1:halo, :] = halo_ref[j, halo + kc * rpc - 1:halo + kc * rpc, :]

    shift_bits = rpc.bit_length() - 1
    row = lax.broadcasted_iota(jnp.int32, (q, q), 0)
    col = lax.broadcasted_iota(jnp.int32, (q, q), 1)
    pos_r = (row & (rpc - 1)) * ph_n + lax.shift_right_logical(row, shift_bits)
    pos_c = (col & (rpc - 1)) * ph_n + lax.shift_right_logical(col, shift_bits)
    causal = pos_r >= pos_c
    cum_mat = (pos_r <= pos_c).astype(F32)
    neg_a = -jnp.exp(alogt_ref[...])
    lane = lax.broadcasted_iota(jnp.int32, (q, hp), 1)
    first = lane < M2_HEADDIM
    first_row = lax.broadcasted_iota(jnp.int32, (1, hp), 1) < M2_HEADDIM
    heads_per_group = heads // M2_GROUPS
    pairs_per_group = heads_per_group // 2

    def core(k):
        xk = xr_ref[k]
        pk = prev_ref[k]
        acc = cb_ref[...] + cw_ref[M2_CONV - 1:M2_CONV, :] * xk
        for s in range(1, M2_CONV):
            shifted = jnp.concatenate([pk[(n_prev - s) * rpc:], xk[:q - s * rpc]], axis=0)
            acc += cw_ref[M2_CONV - 1 - s:M2_CONV - s, :] * shifted
        xbc = acc * _sigmoid(acc)
        dtk = dtt_ref[k]
        a_cst = LOG2_E * jnp.dot(dtk * neg_a, cum_mat, precision=HIGHEST,
                                 preferred_element_type=F32)
        src = a_cst - LOG2_E * jnp.log(dtk)
        a_cs = jnp.concatenate([a_cst, jnp.zeros((q - heads, q), F32)], axis=0).T
        state_w = jnp.exp2(a_cst[:, q - 1:q] - a_cst) * dtk
        chunk_decay = jnp.exp2(a_cs[q - 1:q, :])
        for g in range(M2_GROUPS):
            b_g = xbc[:, d_inner + g * n:d_inner + (g + 1) * n]
            c_g = xbc[:, d_inner + M2_GROUPS * n + g * n:d_inner + M2_GROUPS * n + (g + 1) * n]
            cb = lax.dot_general(c_g.astype(BF16), b_g.astype(BF16), (((1,), (1,)), ((), ())),
                                 preferred_element_type=F32)
            b_t = b_g.T
            for j in range(pairs_per_group):
                h0 = g * heads_per_group + 2 * j
                sl = slice((h0 // 2) * hp, (h0 // 2 + 1) * hp)
                lhs, lhs_state = [], []
                for hd in (h0, h0 + 1):
                    bc = jnp.broadcast_to(a_cs[:, hd:hd + 1], (q, q))
                    decay_dt = jnp.exp2(jnp.where(causal, bc - src[hd:hd + 1, :], -jnp.inf))
                    lhs.append(cb * decay_dt)
                    lhs_state.append(b_t * state_w[hd:hd + 1, :])
                for hd in (h0, h0 + 1):
                    lhs.append(c_g * jnp.exp2(jnp.broadcast_to(a_cs[:, hd:hd + 1], (q, n))))
                xs_p = xbc[:, sl]
                st_p = st_ref[:, sl]
                x0 = jnp.where(first, xs_p, 0.0)
                x1 = jnp.where(first, 0.0, xs_p)
                s0 = jnp.where(first, st_p, 0.0)
                s1 = jnp.where(first, 0.0, st_p)
                lhs = jnp.concatenate(lhs, axis=1).astype(BF16)
                rhs = jnp.concatenate([x0, x1, s0, s1], axis=0).astype(BF16)
                y = jnp.dot(lhs, rhs, preferred_element_type=F32)
                y_ref[k, :, sl] = y + xs_p * dskip_ref[:, sl]
                s_new = jnp.dot(jnp.concatenate(lhs_state, axis=1).astype(BF16),
                                jnp.concatenate([x0, x1], axis=0).astype(BF16),
                                preferred_element_type=F32)
                cd = jnp.where(first_row, chunk_decay[:, h0:h0 + 1], chunk_decay[:, h0 + 1:h0 + 2])
                st_ref[:, sl] = st_p * cd + s_new

    def post(k):
        z = jnp.dot(h_ref[pl.ds(k * q, q), :], wz_ref[...], preferred_element_type=F32)
        y = y_ref[k] * (z * _sigmoid(z))
        gw = d_inner // M2_GROUPS
        parts = []
        for g in range(M2_GROUPS):
            yg = y[:, g * gw:(g + 1) * gw]
            parts.append(yg * lax.rsqrt(jnp.mean(yg * yg, axis=-1, keepdims=True) + NORM_EPS))
        y = jnp.concatenate(parts, axis=1) * ng_ref[...]
        out = jnp.dot(y.astype(BF16), wo_ref[...], preferred_element_type=F32)
        for ph in range(ph_n):
            res = (_strided_rows(tiles, k * q + ph, rpc, ph_n)
                   + gate_ref[0] * out[ph * rpc:(ph + 1) * rpc])
            _set_strided_rows(tiles, k * q + ph, rpc, ph_n, res)

    core(0)

    def step(k, carry):
        post(k - 1)
        core(k)
        return carry

    lax.fori_loop(1, kc, step, 0)
    post(kc - 1)
    _unstage_lane_tiles(tiles, o_ref)


def _ssd_layer(x, norm_g, shift, scale, gate, w_in, conv_w, conv_b, dt_bias, a_log,
               d_skip, gn_g, w_out):
    bsz, seq, d = x.shape
    d_inner = w_out.shape[0]
    heads = d_inner // M2_HEADDIM
    conv_dim = d_inner + 2 * M2_GROUPS * M2_STATE
    q = M2_CHUNK
    kc = M2_BLOCK_CHUNKS
    rows = kc * M2_PHASE_ROWS

    w_z = w_in[:, :d_inner].astype(BF16)
    w_x = w_in[:, d_inner:d_inner + conv_dim].astype(BF16)
    w_dt_t = w_in[:, d_inner + conv_dim:].T.astype(BF16)

    xv = x.reshape(bsz, seq // M2_PHASES, M2_PHASES, d)
    blk = pl.BlockSpec((1, rows, M2_PHASES, d), lambda b, i: (b, i, 0, 0))
    per_b = pl.BlockSpec((1, 1, d), lambda b, i: (b, 0, 0))
    out = pl.pallas_call(
        functools.partial(_ssd_kernel, d_inner=d_inner, heads=heads),
        grid=(bsz, seq // (kc * q)),
        in_specs=[blk, _const_spec((1, d)), per_b, per_b, per_b,
                  _const_spec((d, d_inner)), _const_spec((d, conv_dim)), _const_spec((heads, d)),
                  _const_spec((M2_CONV, conv_dim)), _const_spec((1, conv_dim)),
                  _const_spec((heads, 1)), _const_spec((heads, 1)),
                  _const_spec((1, d_inner)), _const_spec((1, d_inner)), _const_spec((d_inner, d))],
        out_specs=blk,
        out_shape=jax.ShapeDtypeStruct(xv.shape, F32),
        scratch_shapes=[pltpu.VMEM((kc * q, d), BF16),
                        pltpu.VMEM((kc, q, conv_dim), F32),
                        pltpu.VMEM((kc, heads, q), F32),
                        pltpu.VMEM((kc, q, d_inner), F32),
                        pltpu.VMEM((M2_CONV - 1, V7X_SUBLANES + rows, conv_dim), F32),
                        pltpu.VMEM((kc, (M2_CONV - 1) * M2_PHASE_ROWS, conv_dim), F32),
                        pltpu.VMEM((M2_STATE, d_inner), F32)]
        + [pltpu.VMEM((kc * q, V7X_LANES), F32)] * (d // V7X_LANES),
        compiler_params=_compiler_params(
            ("arbitrary", "arbitrary"),
            2 * _nbytes((kc * q, d), F32),
            _nbytes((d, 2 * d_inner + conv_dim), BF16)
            + 2 * _nbytes((kc * q, conv_dim + d_inner), F32)
            + 2 * _nbytes((kc * q, conv_dim), F32)),
        name="ssd_mixer",
    )(xv, norm_g.reshape(1, d), shift, scale, gate, w_z, w_x, w_dt_t,
      conv_w, conv_b.reshape(1, conv_dim), dt_bias.reshape(heads, 1), a_log.reshape(heads, 1),
      jnp.repeat(d_skip, M2_HEADDIM).reshape(1, d_inner), gn_g.reshape(1, d_inner),
      w_out.astype(BF16))
    return out.reshape(bsz, seq, d)


def kernel(x, c, ada_w, ada_b, norm_mix_g, norm_mlp_g, mlp_w1, mlp_w2, s5_w_in, s5_lambda_re, s5_lambda_im, s5_log_dt, s5_b_re, s5_b_im, s5_c_re, s5_c_im, s5_d, s5_w_glu, s5_b_glu, m2_w_in, m2_conv_w, m2_conv_b, m2_dt_bias, m2_a_log, m2_d, m2_norm_g, m2_w_out, final_norm_g):
    depth = ada_w.shape[0]
    bsz, _, d = x.shape
    mod = _ada_modulation(c, ada_w, ada_b).reshape(depth, bsz, N_MOD, 1, d)
    for i in range(depth):
        sh1, sc1, g1, sh2, sc2, g2 = (mod[i, :, k] for k in range(N_MOD))
        j = i // 2
        if i % 2 == 0:
            x = _s5_layer(x, norm_mix_g[i], sh1, sc1, g1, s5_w_in[j], s5_lambda_re[j],
                          s5_lambda_im[j], s5_log_dt[j], s5_b_re[j], s5_b_im[j],
                          s5_c_re[j], s5_c_im[j], s5_d[j], s5_w_glu[j], s5_b_glu[j])
        else:
            x = _ssd_layer(x, norm_mix_g[i], sh1, sc1, g1, m2_w_in[j], m2_conv_w[j],
                           m2_conv_b[j], m2_dt_bias[j], m2_a_log[j], m2_d[j],
                           m2_norm_g[j], m2_w_out[j])
        x = _mlp_layer(x, norm_mlp_g[i], sh2, sc2, g2, mlp_w1[i], mlp_w2[i],
                       final_g=final_norm_g if i == depth - 1 else None)
    return x
```

```python
import functools

import jax
import jax.numpy as jnp
from jax import lax
from jax.experimental import pallas as pl
from jax.experimental.pallas import tpu as pltpu

F32 = jnp.float32
BF16 = jnp.bfloat16
HIGHEST = lax.Precision.HIGHEST
NORM_EPS = 1e-5
N_MOD = 6
LOG2_E = 1.4426950408889634

V7X_LANES = 128
V7X_SUBLANES = 8
V7X_SCOPED_VMEM_BYTES = 60000 * 1024

S5_GROUP = 16
S5_STATE = 64
S5_Q = 16
S5_QW = S5_Q * S5_GROUP
S5_PAIR_STATE = 4 * S5_STATE
S5_IN_CHUNKS = 256
S5_OUT_CHUNKS = 128
S5_OUT_OFFSETS = 4

M2_HEADDIM = 64
M2_GROUPS = 4
M2_STATE = 128
M2_CONV = 4
M2_CHUNK = 128
M2_HEAD_PAIR = 2 * M2_HEADDIM
M2_PHASES = V7X_SUBLANES
M2_PHASE_ROWS = M2_CHUNK // M2_PHASES
M2_BLOCK_CHUNKS = 4

TOKEN_BLOCK = 512
SCAN_BLOCK = 64


def _compiler_params(semantics, block_bytes, temp_bytes):
    want = 2 * block_bytes + temp_bytes
    return pltpu.CompilerParams(
        dimension_semantics=semantics,
        vmem_limit_bytes=int(min(V7X_SCOPED_VMEM_BYTES, max(want, 16 * 1024 * 1024))))


def _nbytes(shape, dtype):
    n = 1
    for s in shape:
        n *= s
    return n * jnp.dtype(dtype).itemsize


def _const_spec(shape):
    return pl.BlockSpec(shape, lambda *_: (0,) * len(shape), pipeline_mode=pl.Buffered(1))


def _rms_mod(x, g, shift, scale):
    y = x * lax.rsqrt(jnp.mean(x * x, axis=-1, keepdims=True) + NORM_EPS)
    return (y * g) * (1.0 + scale) + shift


def _softplus(x):
    return jnp.maximum(x, 0.0) + jnp.log1p(jnp.exp(-jnp.abs(x)))


def _sigmoid(x):
    return 0.5 * jnp.tanh(0.5 * x) + 0.5


def _ada_kernel(c_ref, w_ref, b_ref, o_ref):
    cond = jax.nn.silu(c_ref[...])
    o_ref[0] = jnp.dot(cond, w_ref[0], precision=HIGHEST,
                       preferred_element_type=F32) + b_ref[0]


def _ada_modulation(c, ada_w, ada_b):
    depth, d, n = ada_w.shape
    bsz = c.shape[0]
    rows = V7X_SUBLANES
    nb = n // 4
    c_pad = jnp.zeros((rows, d), F32).at[:bsz].set(c)
    out = pl.pallas_call(
        _ada_kernel,
        grid=(depth, n // nb),
        in_specs=[pl.BlockSpec((rows, d), lambda i, j: (0, 0)),
                  pl.BlockSpec((1, d, nb), lambda i, j: (i, 0, j)),
                  pl.BlockSpec((1, 1, nb), lambda i, j: (i, 0, j))],
        out_specs=pl.BlockSpec((1, rows, nb), lambda i, j: (i, 0, j)),
        out_shape=jax.ShapeDtypeStruct((depth, rows, n), F32),
        compiler_params=_compiler_params(("arbitrary", "arbitrary"),
                                         _nbytes((d, nb), F32), 4 * 1024 * 1024),
        name="ada_modulation",
    )(c_pad, ada_w, ada_b.reshape(depth, 1, n))
    return out[:, :bsz, :]


def _mlp_kernel(x_ref, g_ref, sh_ref, sc_ref, gate_ref, w1_ref, w2_ref, *rest, final):
    if final:
        fg_ref, o_ref = rest
    else:
        (o_ref,) = rest
    x = x_ref[0]
    h = _rms_mod(x, g_ref[...], sh_ref[0], sc_ref[0])
    a = jnp.maximum(jnp.dot(h.astype(BF16), w1_ref[0], preferred_element_type=F32), 0.0)
    y = jnp.dot((a * a).astype(BF16), w2_ref[0], preferred_element_type=F32)
    out = x + gate_ref[0] * y
    if final:
        out = out * lax.rsqrt(jnp.mean(out * out, axis=-1, keepdims=True) + NORM_EPS)
        out = out * fg_ref[...]
    o_ref[0] = out


def _mlp_layer(x, norm_g, shift, scale, gate, w1_all, w2_all, layer, final_g=None):
    bsz, seq, d = x.shape
    f = w1_all.shape[2]
    tl = TOKEN_BLOCK
    tok = pl.BlockSpec((1, tl, d), lambda b, i: (b, i, 0))
    per_b = pl.BlockSpec((1, 1, d), lambda b, i: (b, 0, 0))

    def layer_spec(r, c):
        return pl.BlockSpec((1, r, c), lambda b, i: (layer, 0, 0), pipeline_mode=pl.Buffered(1))

    in_specs = [tok, _const_spec((1, d)), per_b, per_b, per_b, layer_spec(d, f), layer_spec(f, d)]
    args = [x, norm_g.reshape(1, d), shift, scale, gate, w1_all, w2_all]
    if final_g is not None:
        in_specs.append(_const_spec((1, d)))
        args.append(final_g.reshape(1, d))
    return pl.pallas_call(
        functools.partial(_mlp_kernel, final=final_g is not None),
        grid=(bsz, seq // tl),
        in_specs=in_specs,
        out_specs=tok,
        out_shape=jax.ShapeDtypeStruct(x.shape, F32),
        compiler_params=_compiler_params(
            ("parallel", "parallel"),
            2 * _nbytes((tl, d), F32) + _nbytes((d, f), BF16),
            _nbytes((tl, f), F32) * 2 + _nbytes((tl, d), F32) * 2),
        name="mlp",
    )(*args)


def _stage_lane_tiles(x_ref, tiles):
    r, s, _ = x_ref.shape[1:]
    for j, t_ref in enumerate(tiles):
        t_ref[...] = x_ref[0, :, :, j * V7X_LANES:(j + 1) * V7X_LANES].reshape(r * s, V7X_LANES)


def _unstage_lane_tiles(tiles, o_ref):
    r, s, _ = o_ref.shape[1:]
    for j, t_ref in enumerate(tiles):
        o_ref[0, :, :, j * V7X_LANES:(j + 1) * V7X_LANES] = t_ref[...].reshape(r, s, V7X_LANES)


def _strided_rows(tiles, start, size, stride):
    return jnp.concatenate([t[pl.ds(start, size, stride=stride), :] for t in tiles], axis=1)


def _set_strided_rows(tiles, start, size, stride, value):
    for j, t_ref in enumerate(tiles):
        t_ref[pl.ds(start, size, stride=stride), :] = value[:, j * V7X_LANES:(j + 1) * V7X_LANES]


def _s5_in_kernel(x_ref, g_ref, sh_ref, sc_ref, w_ref, u_ref, *tiles):
    nc, tg = x_ref.shape[1:3]
    _stage_lane_tiles(x_ref, tiles)
    for j in range(tg):
        h = _rms_mod(_strided_rows(tiles, j, nc, tg), g_ref[...], sh_ref[0], sc_ref[0])
        u = jnp.dot(h.astype(BF16), w_ref[...], preferred_element_type=F32)
        u_ref[0, j] = u.T.astype(BF16)


def _group_inputs(x_ref, gi):
    q, _, nc = x_ref.shape[1:]
    return x_ref[0, :, gi * S5_GROUP:(gi + 1) * S5_GROUP, :].reshape(q * S5_GROUP, nc)


def _s5_state_kernel(x_ref, ws_ref, s_ref):
    p = S5_STATE
    parts = [jnp.dot(ws_ref[gi], _group_inputs(x_ref, gi), preferred_element_type=F32)
             for gi in range(2)]
    st = jnp.concatenate([parts[0][:p], parts[1][:p], parts[0][p:], parts[1][p:]], axis=0)
    s_ref[...] = st.T


def _s5_scan_kernel(a_ref, s_ref, o_ref, h_ref):
    half = a_ref.shape[1] // 2

    @pl.when(pl.program_id(0) == 0)
    def _():
        h_ref[...] = jnp.zeros_like(h_ref)

    a_re = a_ref[:, :half]
    a_im = a_ref[:, half:]

    def body(c, carry):
        h_re, h_im = carry
        o_ref[c, :, :half] = h_re
        o_ref[c, :, half:] = h_im
        s = s_ref[c]
        return (a_re * h_re - a_im * h_im + s[:, :half],
                a_re * h_im + a_im * h_re + s[:, half:])

    h_re, h_im = lax.fori_loop(0, s_ref.shape[0], body,
                               (h_ref[:, :half], h_ref[:, half:]))
    h_ref[:, :half] = h_re
    h_ref[:, half:] = h_im


def _s5_y_kernel(x_ref, h_ref, t_ref, wy_ref, y_ref):
    q = x_ref.shape[1]
    nc = x_ref.shape[3]
    p = S5_STATE
    ht = h_ref[...].T.astype(BF16)
    for gi in range(2):
        hg = jnp.concatenate([ht[gi * p:(gi + 1) * p], ht[(2 + gi) * p:(3 + gi) * p]], axis=0)
        y = jnp.dot(t_ref[gi], _group_inputs(x_ref, gi), preferred_element_type=F32)
        y += jnp.dot(wy_ref[gi], hg, preferred_element_type=F32)
        y_ref[0, :, gi * S5_GROUP:(gi + 1) * S5_GROUP, :] = y.reshape(q, S5_GROUP, nc)


def _s5_out_kernel(y_ref, x_ref, w_ref, b_ref, gate_ref, o_ref, *tiles):
    d = w_ref.shape[0]
    nt, nc = y_ref.shape[1], y_ref.shape[3]
    tg = S5_OUT_OFFSETS
    _stage_lane_tiles(x_ref, tiles)
    for j0 in range(0, nt, tg):
        g = jnp.concatenate([jax.nn.gelu(y_ref[0, j0 + j]).T for j in range(tg)], axis=0)
        ab = jnp.dot(g.astype(BF16), w_ref[...], preferred_element_type=F32) + b_ref[...]
        mix = ab[:, :d] * _sigmoid(ab[:, d:])
        for j in range(tg):
            out = (_strided_rows(tiles, j0 + j, nc, nt)
                   + gate_ref[0] * mix[j * nc:(j + 1) * nc])
            _set_strided_rows(tiles, j0 + j, nc, nt, out)
    _unstage_lane_tiles(tiles, o_ref)


def _s5_tables(lam_re, lam_im, log_dt, b_re, b_im, c_re, c_im, d_skip):
    g, p = lam_re.shape
    q, hh = S5_Q, S5_GROUP
    dt = jnp.exp(log_dt)[:, None]
    ld_re, ld_im = lam_re * dt, lam_im * dt

    def lam_pow(k):
        mag = jnp.exp(ld_re[..., None] * k)
        ang = ld_im[..., None] * k
        return mag * jnp.cos(ang), mag * jnp.sin(ang)

    pw_re, pw_im = lam_pow(jnp.arange(q + 1, dtype=F32))
    lb_re, lb_im = pw_re[..., 1], pw_im[..., 1]
    den = lam_re * lam_re + lam_im * lam_im
    f_re = ((lb_re - 1.0) * lam_re + lb_im * lam_im) / den
    f_im = (lb_im * lam_re - (lb_re - 1.0) * lam_im) / den
    bb_re = f_re[..., None] * b_re - f_im[..., None] * b_im
    bb_im = f_re[..., None] * b_im + f_im[..., None] * b_re

    kp_re = jnp.transpose(pw_re[..., :q], (0, 2, 1))[:, :, None, :]
    kp_im = jnp.transpose(pw_im[..., :q], (0, 2, 1))[:, :, None, :]
    cp_re = c_re[:, None] * kp_re - c_im[:, None] * kp_im
    cp_im = c_re[:, None] * kp_im + c_im[:, None] * kp_re
    lag = (jnp.einsum('gkop,gpi->gkoi', cp_re, bb_re, precision=HIGHEST)
           - jnp.einsum('gkop,gpi->gkoi', cp_im, bb_im, precision=HIGHEST))
    lag = lag.at[:, 0].add(d_skip.reshape(g, hh)[:, :, None] * jnp.eye(hh, dtype=F32))
    tt = jnp.arange(q)
    sub_diag = (tt[None, :, None] - tt[None, None, :] == tt[:, None, None]).astype(F32)
    toep = jnp.einsum('kut,gkoh->guoth', sub_diag, lag, precision=HIGHEST).reshape(g, q * hh, q * hh)

    rv_re = jnp.repeat(pw_re[..., :q][..., ::-1], hh, axis=-1)
    rv_im = jnp.repeat(pw_im[..., :q][..., ::-1], hh, axis=-1)
    bt_re, bt_im = jnp.tile(bb_re, (1, 1, q)), jnp.tile(bb_im, (1, 1, q))
    ws = jnp.concatenate([rv_re * bt_re - rv_im * bt_im, rv_re * bt_im + rv_im * bt_re], axis=1)

    up_re = jnp.transpose(pw_re[..., 1:], (0, 2, 1))[:, :, None, :]
    up_im = jnp.transpose(pw_im[..., 1:], (0, 2, 1))[:, :, None, :]
    cy_re = (c_re[:, None] * up_re - c_im[:, None] * up_im).reshape(g, q * hh, p)
    cy_im = (c_re[:, None] * up_im + c_im[:, None] * up_re).reshape(g, q * hh, p)
    wy = jnp.concatenate([cy_re, -cy_im], axis=2)

    a_p = jnp.concatenate([pw_re[..., q].reshape(g // 2, 2 * p),
                           pw_im[..., q].reshape(g // 2, 2 * p)], axis=1)
    return toep.astype(BF16), ws.astype(BF16), wy.astype(BF16), a_p


def _s5_layer(x, norm_g, shift, scale, gate, w_in, lam_re, lam_im, log_dt,
              b_re, b_im, c_re, c_im, d_skip, w_glu, b_glu):
    bsz, seq, d = x.shape
    groups = d // S5_GROUP
    pairs = groups // 2
    nc = seq // S5_Q
    pair_rows = 2 * S5_GROUP
    per_b = pl.BlockSpec((1, 1, d), lambda b, i, k: (b, 0, 0))

    xv = x.reshape(bsz, nc, S5_Q, d)
    tg = V7X_SUBLANES

    def tok_spec(ncb):
        return pl.BlockSpec((1, ncb, tg, d), lambda b, i, k: (b, i, k, 0))

    def chan_spec(ncb):
        return pl.BlockSpec((1, tg, d, ncb), lambda b, i, k: (b, k, 0, i))

    def lane_tiles(ncb):
        return [pltpu.VMEM((ncb * tg, V7X_LANES), F32)] * (d // V7X_LANES)

    ncb = S5_IN_CHUNKS
    ut = pl.pallas_call(
        _s5_in_kernel,
        grid=(bsz, nc // ncb, S5_Q // tg),
        in_specs=[tok_spec(ncb), _const_spec((1, d)), per_b, per_b, _const_spec((d, d))],
        out_specs=chan_spec(ncb),
        out_shape=jax.ShapeDtypeStruct((bsz, S5_Q, d, nc), BF16),
        scratch_shapes=lane_tiles(ncb),
        compiler_params=_compiler_params(("parallel", "parallel", "parallel"),
                                         _nbytes((ncb, tg, d), F32) + _nbytes((tg, d, ncb), BF16),
                                         _nbytes((d, d), BF16) + _nbytes((ncb, tg, d), F32)
                                         + 8 * _nbytes((ncb, d), F32)),
        name="s5_in",
    )(xv, norm_g.reshape(1, d), shift, scale, w_in.astype(BF16))

    toep, ws, wy, a_pair = _s5_tables(lam_re, lam_im, log_dt, b_re, b_im, c_re, c_im, d_skip)

    rows = bsz * pairs
    xspec = pl.BlockSpec((1, S5_Q, pair_rows, nc), lambda b, p: (b, 0, p, 0))
    sspec = pl.BlockSpec((nc, S5_PAIR_STATE), lambda b, p: (0, b * pairs + p))

    def table_spec(r, c):
        return pl.BlockSpec((2, r, c), lambda b, p: (p, 0, 0))

    s_loc = pl.pallas_call(
        _s5_state_kernel,
        grid=(bsz, pairs),
        in_specs=[xspec, table_spec(2 * S5_STATE, S5_QW)],
        out_specs=sspec,
        out_shape=jax.ShapeDtypeStruct((nc, rows * S5_PAIR_STATE), F32),
        compiler_params=_compiler_params(
            ("parallel", "parallel"),
            _nbytes((nc, 2 * S5_QW), BF16) + _nbytes((nc, S5_PAIR_STATE), F32),
            3 * _nbytes((nc, S5_PAIR_STATE), F32)),
        name="s5_chunk_state",
    )(ut, ws)

    cb = SCAN_BLOCK
    sblk = pl.BlockSpec((cb, rows, S5_PAIR_STATE), lambda i: (i, 0, 0))
    h_in = pl.pallas_call(
        _s5_scan_kernel,
        grid=(nc // cb,),
        in_specs=[_const_spec((rows, S5_PAIR_STATE)), sblk],
        out_specs=sblk,
        out_shape=jax.ShapeDtypeStruct((nc, rows, S5_PAIR_STATE), F32),
        scratch_shapes=[pltpu.VMEM((rows, S5_PAIR_STATE), F32)],
        compiler_params=_compiler_params(("arbitrary",),
                                         2 * _nbytes((cb, rows, S5_PAIR_STATE), F32),
                                         1024 * 1024),
        name="s5_chunk_scan",
    )(jnp.tile(a_pair, (bsz, 1)), s_loc.reshape(nc, rows, S5_PAIR_STATE))

    yt = pl.pallas_call(
        _s5_y_kernel,
        grid=(bsz, pairs),
        in_specs=[xspec, sspec, table_spec(S5_QW, S5_QW), table_spec(S5_QW, 2 * S5_STATE)],
        out_specs=xspec,
        out_shape=jax.ShapeDtypeStruct((bsz, S5_Q, d, nc), F32),
        compiler_params=_compiler_params(
            ("parallel", "parallel"),
            _nbytes((nc, 2 * S5_QW), BF16) + _nbytes((nc, 2 * S5_QW), F32)
            + _nbytes((nc, S5_PAIR_STATE), F32),
            3 * _nbytes((nc, 2 * S5_QW), F32)),
        name="s5_chunk_output",
    )(ut, h_in.reshape(nc, rows * S5_PAIR_STATE), toep, wy)

    ncb = S5_OUT_CHUNKS
    out = pl.pallas_call(
        _s5_out_kernel,
        grid=(bsz, nc // ncb, S5_Q // tg),
        in_specs=[chan_spec(ncb), tok_spec(ncb), _const_spec((d, 2 * d)),
                  _const_spec((1, 2 * d)), per_b],
        out_specs=tok_spec(ncb),
        out_shape=jax.ShapeDtypeStruct(xv.shape, F32),
        scratch_shapes=lane_tiles(ncb),
        compiler_params=_compiler_params(
            ("parallel", "parallel", "parallel"),
            3 * _nbytes((ncb, tg, d), F32),
            _nbytes((d, 2 * d), BF16) + _nbytes((ncb, tg, d), F32)
            + 6 * _nbytes((S5_OUT_OFFSETS * ncb, 2 * d), F32)),
        name="s5_out",
    )(yt, xv, w_glu.astype(BF16), b_glu.reshape(1, 2 * d), gate)
    return out.reshape(bsz, seq, d)


def _ssd_kernel(x_ref, g_ref, sh_ref, sc_ref, gate_ref, win_ref, wdtt_ref, cw_ref, cb_ref,
                biast_ref, alogt_ref, dskip_ref, ng_ref, wo_ref, o_ref,
                h_ref, xr_ref, dtt_ref, y_ref, halo_ref, prev_ref, st_ref, *tiles, d_inner, heads):
    q = M2_CHUNK
    n = M2_STATE
    hp = M2_HEAD_PAIR
    ph_n = M2_PHASES
    rpc = M2_PHASE_ROWS
    kc = M2_BLOCK_CHUNKS
    halo = V7X_SUBLANES
    n_prev = M2_CONV - 1

    @pl.when(pl.program_id(1) == 0)
    def _():
        halo_ref[:, halo - 1:halo, :] = jnp.zeros((n_prev, 1, halo_ref.shape[2]), F32)
        st_ref[...] = jnp.zeros_like(st_ref)

    _stage_lane_tiles(x_ref, tiles)
    xb = jnp.concatenate([_strided_rows(tiles, k * q + ph, rpc, ph_n)
                          for k in range(kc) for ph in range(ph_n)], axis=0)
    h = _rms_mod(xb, g_ref[...], sh_ref[0], sc_ref[0]).astype(BF16)
    h_ref[...] = h
    conv_dim = xr_ref.shape[2]
    xr = jnp.dot(h, win_ref[:, d_inner:d_inner + conv_dim], preferred_element_type=F32)
    dtt = _softplus(lax.dot_general(wdtt_ref[...], h, (((1,), (1,)), ((), ())),
                                    preferred_element_type=F32) + biast_ref[...])
    for k in range(kc):
        dtt_ref[k] = dtt[:, k * q:(k + 1) * q]

    for k in range(kc):
        xr_ref[k] = xr[k * q:(k + 1) * q, :]
    for j in range(n_prev):
        ph = ph_n - n_prev + j
        for k in range(kc):
            halo_ref[j, halo + k * rpc:halo + (k + 1) * rpc, :] = (
                xr[k * q + ph * rpc:k * q + (ph + 1) * rpc, :])
    for j in range(n_prev):
        back = halo_ref[j, pl.ds(halo - 1, kc * rpc), :]
        for k in range(kc):
            prev_ref[k, j * rpc:(j + 1) * rpc, :] = back[k * rpc:(k + 1) * rpc]
        halo_ref[j, halo - 1:halo, :] = halo_ref[j, halo + kc * rpc - 1:halo + kc * rpc, :]

    shift_bits = rpc.bit_length() - 1
    row = lax.broadcasted_iota(jnp.int32, (q, q), 0)
    col = lax.broadcasted_iota(jnp.int32, (q, q), 1)
    pos_r = (row & (rpc - 1)) * ph_n + lax.shift_right_logical(row, shift_bits)
    pos_c = (col & (rpc - 1)) * ph_n + lax.shift_right_logical(col, shift_bits)
    causal = pos_r >= pos_c
    cum_mat = (pos_r <= pos_c).astype(F32)
    neg_a = -jnp.exp(alogt_ref[...])
    lane = lax.broadcasted_iota(jnp.int32, (q, hp), 1)
    first = lane < M2_HEADDIM
    first_row = lax.broadcasted_iota(jnp.int32, (1, hp), 1) < M2_HEADDIM
    heads_per_group = heads // M2_GROUPS
    pairs_per_group = heads_per_group // 2

    def core(k):
        xk = xr_ref[k]
        pk = prev_ref[k]
        acc = cb_ref[...] + cw_ref[M2_CONV - 1:M2_CONV, :] * xk
        for s in range(1, M2_CONV):
            shifted = jnp.concatenate([pk[(n_prev - s) * rpc:], xk[:q - s * rpc]], axis=0)
            acc += cw_ref[M2_CONV - 1 - s:M2_CONV - s, :] * shifted
        xbc = acc * _sigmoid(acc)
        dtk = dtt_ref[k]
        a_cst = LOG2_E * jnp.dot(dtk * neg_a, cum_mat, precision=HIGHEST,
                                 preferred_element_type=F32)
        src = a_cst - LOG2_E * jnp.log(dtk)
        a_cs = jnp.concatenate([a_cst, jnp.zeros((q - heads, q), F32)], axis=0).T
        state_w = jnp.exp2(a_cst[:, q - 1:q] - a_cst) * dtk
        chunk_decay = jnp.exp2(a_cs[q - 1:q, :])
        for g in range(M2_GROUPS):
            b_g = xbc[:, d_inner + g * n:d_inner + (g + 1) * n]
            c_g = xbc[:, d_inner + M2_GROUPS * n + g * n:d_inner + M2_GROUPS * n + (g + 1) * n]
            cb = lax.dot_general(c_g.astype(BF16), b_g.astype(BF16), (((1,), (1,)), ((), ())),
                                 preferred_element_type=F32)
            b_t = b_g.T
            for j in range(pairs_per_group):
                h0 = g * heads_per_group + 2 * j
                sl = slice((h0 // 2) * hp, (h0 // 2 + 1) * hp)
                lhs, lhs_state = [], []
                for hd in (h0, h0 + 1):
                    bc = jnp.broadcast_to(a_cs[:, hd:hd + 1], (q, q))
                    decay_dt = jnp.exp2(jnp.where(causal, bc - src[hd:hd + 1, :], -jnp.inf))
                    lhs.append(cb * decay_dt)
                    lhs_state.append(b_t * state_w[hd:hd + 1, :])
                for hd in (h0, h0 + 1):
                    lhs.append(c_g * jnp.exp2(jnp.broadcast_to(a_cs[:, hd:hd + 1], (q, n))))
                xs_p = xbc[:, sl]
                st_p = st_ref[:, sl]
                x0 = jnp.where(first, xs_p, 0.0)
                x1 = jnp.where(first, 0.0, xs_p)
                s0 = jnp.where(first, st_p, 0.0)
                s1 = jnp.where(first, 0.0, st_p)
                lhs = jnp.concatenate(lhs, axis=1).astype(BF16)
                rhs = jnp.concatenate([x0, x1, s0, s1], axis=0).astype(BF16)
                y = jnp.dot(lhs, rhs, preferred_element_type=F32)
                y_ref[k, :, sl] = y + xs_p * dskip_ref[:, sl]
                s_new = jnp.dot(jnp.concatenate(lhs_state, axis=1).astype(BF16),
                                jnp.concatenate([x0, x1], axis=0).astype(BF16),
                                preferred_element_type=F32)
                cd = jnp.where(first_row, chunk_decay[:, h0:h0 + 1], chunk_decay[:, h0 + 1:h0 + 2])
                st_ref[:, sl] = st_p * cd + s_new

    def post(k):
        z = jnp.dot(h_ref[pl.ds(k * q, q), :], win_ref[:, :d_inner],
                    preferred_element_type=F32)
        y = y_ref[k] * (z * _sigmoid(z))
        gw = d_inner // M2_GROUPS
        parts = []
        for g in range(M2_GROUPS):
            yg = y[:, g * gw:(g + 1) * gw]
            parts.append(yg * lax.rsqrt(jnp.mean(yg * yg, axis=-1, keepdims=True) + NORM_EPS))
        y = jnp.concatenate(parts, axis=1) * ng_ref[...]
        out = jnp.dot(y.astype(BF16), wo_ref[...], preferred_element_type=F32)
        for ph in range(ph_n):
            res = (_strided_rows(tiles, k * q + ph, rpc, ph_n)
                   + gate_ref[0] * out[ph * rpc:(ph + 1) * rpc])
            _set_strided_rows(tiles, k * q + ph, rpc, ph_n, res)

    core(0)

    def step(k, carry):
        post(k - 1)
        core(k)
        return carry

    lax.fori_loop(1, kc, step, 0)
    post(kc - 1)
    _unstage_lane_tiles(tiles, o_ref)


def _ssd_layer(x, norm_g, shift, scale, gate, w_in, conv_w, conv_b, dt_bias, a_log,
               d_skip, gn_g, w_out):
    bsz, seq, d = x.shape
    d_inner = w_out.shape[0]
    heads = d_inner // M2_HEADDIM
    conv_dim = d_inner + 2 * M2_GROUPS * M2_STATE
    q = M2_CHUNK
    kc = M2_BLOCK_CHUNKS
    rows = kc * M2_PHASE_ROWS

    w_dt_t = w_in[:, d_inner + conv_dim:].T.astype(BF16)

    xv = x.reshape(bsz, seq // M2_PHASES, M2_PHASES, d)
    blk = pl.BlockSpec((1, rows, M2_PHASES, d), lambda b, i: (b, i, 0, 0))
    per_b = pl.BlockSpec((1, 1, d), lambda b, i: (b, 0, 0))
    out = pl.pallas_call(
        functools.partial(_ssd_kernel, d_inner=d_inner, heads=heads),
        grid=(bsz, seq // (kc * q)),
        in_specs=[blk, _const_spec((1, d)), per_b, per_b, per_b,
                  _const_spec(w_in.shape), _const_spec((heads, d)),
                  _const_spec((M2_CONV, conv_dim)), _const_spec((1, conv_dim)),
                  _const_spec((heads, 1)), _const_spec((heads, 1)),
                  _const_spec((1, d_inner)), _const_spec((1, d_inner)), _const_spec((d_inner, d))],
        out_specs=blk,
        out_shape=jax.ShapeDtypeStruct(xv.shape, F32),
        scratch_shapes=[pltpu.VMEM((kc * q, d), BF16),
                        pltpu.VMEM((kc, q, conv_dim), F32),
                        pltpu.VMEM((kc, heads, q), F32),
                        pltpu.VMEM((kc, q, d_inner), F32),
                        pltpu.VMEM((M2_CONV - 1, V7X_SUBLANES + rows, conv_dim), F32),
                        pltpu.VMEM((kc, (M2_CONV - 1) * M2_PHASE_ROWS, conv_dim), F32),
                        pltpu.VMEM((M2_STATE, d_inner), F32)]
        + [pltpu.VMEM((kc * q, V7X_LANES), F32)] * (d // V7X_LANES),
        compiler_params=_compiler_params(
            ("arbitrary", "arbitrary"),
            2 * _nbytes((kc * q, d), F32),
            _nbytes((d, 2 * d_inner + conv_dim), BF16)
            + 2 * _nbytes((kc * q, conv_dim + d_inner), F32)
            + 2 * _nbytes((kc * q, conv_dim), F32)),
        name="ssd_mixer",
    )(xv, norm_g.reshape(1, d), shift, scale, gate, w_in.astype(BF16), w_dt_t,
      conv_w, conv_b.reshape(1, conv_dim), dt_bias.reshape(heads, 1), a_log.reshape(heads, 1),
      jnp.repeat(d_skip, M2_HEADDIM).reshape(1, d_inner), gn_g.reshape(1, d_inner),
      w_out.astype(BF16))
    return out.reshape(bsz, seq, d)


def kernel(x, c, ada_w, ada_b, norm_mix_g, norm_mlp_g, mlp_w1, mlp_w2, s5_w_in, s5_lambda_re, s5_lambda_im, s5_log_dt, s5_b_re, s5_b_im, s5_c_re, s5_c_im, s5_d, s5_w_glu, s5_b_glu, m2_w_in, m2_conv_w, m2_conv_b, m2_dt_bias, m2_a_log, m2_d, m2_norm_g, m2_w_out, final_norm_g):
    depth = ada_w.shape[0]
    bsz, _, d = x.shape
    mod = _ada_modulation(c, ada_w, ada_b).reshape(depth, bsz, N_MOD, 1, d)
    w1_all, w2_all = mlp_w1.astype(BF16), mlp_w2.astype(BF16)
    for i in range(depth):
        sh1, sc1, g1, sh2, sc2, g2 = (mod[i, :, k] for k in range(N_MOD))
        j = i // 2
        if i % 2 == 0:
            x = _s5_layer(x, norm_mix_g[i], sh1, sc1, g1, s5_w_in[j], s5_lambda_re[j],
                          s5_lambda_im[j], s5_log_dt[j], s5_b_re[j], s5_b_im[j],
                          s5_c_re[j], s5_c_im[j], s5_d[j], s5_w_glu[j], s5_b_glu[j])
        else:
            x = _ssd_layer(x, norm_mix_g[i], sh1, sc1, g1, m2_w_in[j], m2_conv_w[j],
                           m2_conv_b[j], m2_dt_bias[j], m2_a_log[j], m2_d[j],
                           m2_norm_g[j], m2_w_out[j])
        x = _mlp_layer(x, norm_mlp_g[i], sh2, sc2, g2, w1_all, w2_all, i,
                       final_g=final_norm_g if i == depth - 1 else None)
    return x
```

```python
import functools

import jax
import jax.numpy as jnp
from jax import lax
from jax.experimental import pallas as pl
from jax.experimental.pallas import tpu as pltpu

F32 = jnp.float32
BF16 = jnp.bfloat16
HIGHEST = lax.Precision.HIGHEST
NORM_EPS = 1e-5
N_MOD = 6
LOG2_E = 1.4426950408889634

V7X_LANES = 128
V7X_SUBLANES = 8
V7X_SCOPED_VMEM_BYTES = 60000 * 1024

S5_GROUP = 16
S5_STATE = 64
S5_Q = 16
S5_QW = S5_Q * S5_GROUP
S5_PAIR_STATE = 4 * S5_STATE
S5_IN_CHUNKS = 256
S5_OUT_CHUNKS = 128
S5_OUT_OFFSETS = 4

M2_HEADDIM = 64
M2_GROUPS = 4
M2_STATE = 128
M2_CONV = 4
M2_CHUNK = 128
M2_HEAD_PAIR = 2 * M2_HEADDIM
M2_PHASES = V7X_SUBLANES
M2_PHASE_ROWS = M2_CHUNK // M2_PHASES
M2_BLOCK_CHUNKS = 4

TOKEN_BLOCK = 512
SCAN_BLOCK = 64


def _compiler_params(semantics, block_bytes, temp_bytes):
    want = 2 * block_bytes + temp_bytes
    return pltpu.CompilerParams(
        dimension_semantics=semantics,
        vmem_limit_bytes=int(min(V7X_SCOPED_VMEM_BYTES, max(want, 16 * 1024 * 1024))))


def _nbytes(shape, dtype):
    n = 1
    for s in shape:
        n *= s
    return n * jnp.dtype(dtype).itemsize


def _const_spec(shape):
    return pl.BlockSpec(shape, lambda *_: (0,) * len(shape), pipeline_mode=pl.Buffered(1))


def _rms_mod(x, g, shift, scale):
    y = x * lax.rsqrt(jnp.mean(x * x, axis=-1, keepdims=True) + NORM_EPS)
    return (y * g) * (1.0 + scale) + shift


def _softplus(x):
    return jnp.maximum(x, 0.0) + jnp.log1p(jnp.exp(-jnp.abs(x)))


def _sigmoid(x):
    return 0.5 * jnp.tanh(0.5 * x) + 0.5


def _ada_kernel(c_ref, w_ref, b_ref, o_ref):
    cond = jax.nn.silu(c_ref[...])
    o_ref[0] = jnp.dot(cond, w_ref[0], precision=HIGHEST,
                       preferred_element_type=F32) + b_ref[0]


def _ada_modulation(c, ada_w, ada_b):
    depth, d, n = ada_w.shape
    bsz = c.shape[0]
    rows = V7X_SUBLANES
    nb = n // 4
    c_pad = jnp.zeros((rows, d), F32).at[:bsz].set(c)
    out = pl.pallas_call(
        _ada_kernel,
        grid=(depth, n // nb),
        in_specs=[pl.BlockSpec((rows, d), lambda i, j: (0, 0)),
                  pl.BlockSpec((1, d, nb), lambda i, j: (i, 0, j)),
                  pl.BlockSpec((1, 1, nb), lambda i, j: (i, 0, j))],
        out_specs=pl.BlockSpec((1, rows, nb), lambda i, j: (i, 0, j)),
        out_shape=jax.ShapeDtypeStruct((depth, rows, n), F32),
        compiler_params=_compiler_params(("arbitrary", "arbitrary"),
                                         _nbytes((d, nb), F32), 4 * 1024 * 1024),
        name="ada_modulation",
    )(c_pad, ada_w, ada_b.reshape(depth, 1, n))
    return out[:, :bsz, :]


def _mlp_kernel(x_ref, g_ref, sh_ref, sc_ref, gate_ref, w1_ref, w2_ref, *rest, final):
    if final:
        fg_ref, o_ref = rest
    else:
        (o_ref,) = rest
    x = x_ref[0]
    h = _rms_mod(x, g_ref[...], sh_ref[0], sc_ref[0])
    a = jnp.maximum(jnp.dot(h.astype(BF16), w1_ref[0], preferred_element_type=F32), 0.0)
    y = jnp.dot((a * a).astype(BF16), w2_ref[0], preferred_element_type=F32)
    out = x + gate_ref[0] * y
    if final:
        out = out * lax.rsqrt(jnp.mean(out * out, axis=-1, keepdims=True) + NORM_EPS)
        out = out * fg_ref[...]
    o_ref[0] = out


def _mlp_layer(x, norm_g, shift, scale, gate, w1_all, w2_all, layer, final_g=None):
    bsz, seq, d = x.shape
    f = w1_all.shape[2]
    tl = TOKEN_BLOCK
    tok = pl.BlockSpec((1, tl, d), lambda b, i: (b, i, 0))
    per_b = pl.BlockSpec((1, 1, d), lambda b, i: (b, 0, 0))

    def layer_spec(r, c):
        return pl.BlockSpec((1, r, c), lambda b, i: (layer, 0, 0), pipeline_mode=pl.Buffered(1))

    in_specs = [tok, _const_spec((1, d)), per_b, per_b, per_b, layer_spec(d, f), layer_spec(f, d)]
    args = [x, norm_g.reshape(1, d), shift, scale, gate, w1_all, w2_all]
    if final_g is not None:
        in_specs.append(_const_spec((1, d)))
        args.append(final_g.reshape(1, d))
    return pl.pallas_call(
        functools.partial(_mlp_kernel, final=final_g is not None),
        grid=(bsz, seq // tl),
        in_specs=in_specs,
        out_specs=tok,
        out_shape=jax.ShapeDtypeStruct(x.shape, F32),
        compiler_params=_compiler_params(
            ("parallel", "parallel"),
            2 * _nbytes((tl, d), F32) + _nbytes((d, f), BF16),
            _nbytes((tl, f), F32) * 2 + _nbytes((tl, d), F32) * 2),
        name="mlp",
    )(*args)


def _stage_lane_tiles(x_ref, tiles):
    r, s, _ = x_ref.shape[1:]
    for j, t_ref in enumerate(tiles):
        t_ref[...] = x_ref[0, :, :, j * V7X_LANES:(j + 1) * V7X_LANES].reshape(r * s, V7X_LANES)


def _unstage_lane_tiles(tiles, o_ref):
    r, s, _ = o_ref.shape[1:]
    for j, t_ref in enumerate(tiles):
        o_ref[0, :, :, j * V7X_LANES:(j + 1) * V7X_LANES] = t_ref[...].reshape(r, s, V7X_LANES)


def _strided_rows(tiles, start, size, stride):
    return jnp.concatenate([t[pl.ds(start, size, stride=stride), :] for t in tiles], axis=1)


def _set_strided_rows(tiles, start, size, stride, value):
    for j, t_ref in enumerate(tiles):
        t_ref[pl.ds(start, size, stride=stride), :] = value[:, j * V7X_LANES:(j + 1) * V7X_LANES]


def _s5_in_kernel(x_ref, g_ref, sh_ref, sc_ref, w_ref, u_ref, *tiles):
    nc, tg = x_ref.shape[1:3]
    _stage_lane_tiles(x_ref, tiles)
    for j in range(tg):
        h = _rms_mod(_strided_rows(tiles, j, nc, tg), g_ref[...], sh_ref[0], sc_ref[0])
        u = jnp.dot(h.astype(BF16), w_ref[...], preferred_element_type=F32)
        u_ref[0, j] = u.T.astype(BF16)


def _group_inputs(x_ref, gi):
    q, _, nc = x_ref.shape[1:]
    return x_ref[0, :, gi * S5_GROUP:(gi + 1) * S5_GROUP, :].reshape(q * S5_GROUP, nc)


def _s5_state_kernel(x_ref, ws_ref, s_ref):
    p = S5_STATE
    parts = [jnp.dot(ws_ref[gi], _group_inputs(x_ref, gi), preferred_element_type=F32)
             for gi in range(2)]
    st = jnp.concatenate([parts[0][:p], parts[1][:p], parts[0][p:], parts[1][p:]], axis=0)
    s_ref[...] = st.T


def _s5_scan_kernel(a_ref, s_ref, o_ref, h_ref):
    half = a_ref.shape[1] // 2

    @pl.when(pl.program_id(0) == 0)
    def _():
        h_ref[...] = jnp.zeros_like(h_ref)

    a_re = a_ref[:, :half]
    a_im = a_ref[:, half:]

    def body(c, carry):
        h_re, h_im = carry
        o_ref[c, :, :half] = h_re
        o_ref[c, :, half:] = h_im
        s = s_ref[c]
        return (a_re * h_re - a_im * h_im + s[:, :half],
                a_re * h_im + a_im * h_re + s[:, half:])

    h_re, h_im = lax.fori_loop(0, s_ref.shape[0], body,
                               (h_ref[:, :half], h_ref[:, half:]))
    h_ref[:, :half] = h_re
    h_ref[:, half:] = h_im


def _s5_y_kernel(x_ref, h_ref, t_ref, wy_ref, y_ref):
    q = x_ref.shape[1]
    nc = x_ref.shape[3]
    p = S5_STATE
    ht = h_ref[...].T.astype(BF16)
    for gi in range(2):
        hg = jnp.concatenate([ht[gi * p:(gi + 1) * p], ht[(2 + gi) * p:(3 + gi) * p]], axis=0)
        y = jnp.dot(t_ref[gi], _group_inputs(x_ref, gi), preferred_element_type=F32)
        y += jnp.dot(wy_ref[gi], hg, preferred_element_type=F32)
        y_ref[0, :, gi * S5_GROUP:(gi + 1) * S5_GROUP, :] = (
            y.reshape(q, S5_GROUP, nc).astype(y_ref.dtype))


def _s5_out_kernel(y_ref, x_ref, w_ref, b_ref, gate_ref, o_ref, *tiles):
    d = w_ref.shape[0]
    nt, nc = y_ref.shape[1], y_ref.shape[3]
    tg = S5_OUT_OFFSETS
    _stage_lane_tiles(x_ref, tiles)
    for j0 in range(0, nt, tg):
        g = jnp.concatenate([jax.nn.gelu(y_ref[0, j0 + j].astype(F32)).T for j in range(tg)],
                            axis=0)
        ab = jnp.dot(g.astype(BF16), w_ref[...], preferred_element_type=F32) + b_ref[...]
        mix = ab[:, :d] * _sigmoid(ab[:, d:])
        for j in range(tg):
            out = (_strided_rows(tiles, j0 + j, nc, nt)
                   + gate_ref[0] * mix[j * nc:(j + 1) * nc])
            _set_strided_rows(tiles, j0 + j, nc, nt, out)
    _unstage_lane_tiles(tiles, o_ref)


def _s5_tables(lam_re, lam_im, log_dt, b_re, b_im, c_re, c_im, d_skip):
    g, p = lam_re.shape
    q, hh = S5_Q, S5_GROUP
    dt = jnp.exp(log_dt)[:, None]
    ld_re, ld_im = lam_re * dt, lam_im * dt

    def lam_pow(k):
        mag = jnp.exp(ld_re[..., None] * k)
        ang = ld_im[..., None] * k
        return mag * jnp.cos(ang), mag * jnp.sin(ang)

    pw_re, pw_im = lam_pow(jnp.arange(q + 1, dtype=F32))
    lb_re, lb_im = pw_re[..., 1], pw_im[..., 1]
    den = lam_re * lam_re + lam_im * lam_im
    f_re = ((lb_re - 1.0) * lam_re + lb_im * lam_im) / den
    f_im = (lb_im * lam_re - (lb_re - 1.0) * lam_im) / den
    bb_re = f_re[..., None] * b_re - f_im[..., None] * b_im
    bb_im = f_re[..., None] * b_im + f_im[..., None] * b_re

    kp_re = jnp.transpose(pw_re[..., :q], (0, 2, 1))[:, :, None, :]
    kp_im = jnp.transpose(pw_im[..., :q], (0, 2, 1))[:, :, None, :]
    cp_re = c_re[:, None] * kp_re - c_im[:, None] * kp_im
    cp_im = c_re[:, None] * kp_im + c_im[:, None] * kp_re
    lag = (jnp.einsum('gkop,gpi->gkoi', cp_re, bb_re, precision=HIGHEST)
           - jnp.einsum('gkop,gpi->gkoi', cp_im, bb_im, precision=HIGHEST))
    lag = lag.at[:, 0].add(d_skip.reshape(g, hh)[:, :, None] * jnp.eye(hh, dtype=F32))
    tt = jnp.arange(q)
    sub_diag = (tt[None, :, None] - tt[None, None, :] == tt[:, None, None]).astype(F32)
    toep = jnp.einsum('kut,gkoh->guoth', sub_diag, lag, precision=HIGHEST).reshape(g, q * hh, q * hh)

    rv_re = jnp.repeat(pw_re[..., :q][..., ::-1], hh, axis=-1)
    rv_im = jnp.repeat(pw_im[..., :q][..., ::-1], hh, axis=-1)
    bt_re, bt_im = jnp.tile(bb_re, (1, 1, q)), jnp.tile(bb_im, (1, 1, q))
    ws = jnp.concatenate([rv_re * bt_re - rv_im * bt_im, rv_re * bt_im + rv_im * bt_re], axis=1)

    up_re = jnp.transpose(pw_re[..., 1:], (0, 2, 1))[:, :, None, :]
    up_im = jnp.transpose(pw_im[..., 1:], (0, 2, 1))[:, :, None, :]
    cy_re = (c_re[:, None] * up_re - c_im[:, None] * up_im).reshape(g, q * hh, p)
    cy_im = (c_re[:, None] * up_im + c_im[:, None] * up_re).reshape(g, q * hh, p)
    wy = jnp.concatenate([cy_re, -cy_im], axis=2)

    a_p = jnp.concatenate([pw_re[..., q].reshape(g // 2, 2 * p),
                           pw_im[..., q].reshape(g // 2, 2 * p)], axis=1)
    return toep.astype(BF16), ws.astype(BF16), wy.astype(BF16), a_p


def _s5_layer(x, norm_g, shift, scale, gate, w_in, lam_re, lam_im, log_dt,
              b_re, b_im, c_re, c_im, d_skip, w_glu, b_glu):
    bsz, seq, d = x.shape
    groups = d // S5_GROUP
    pairs = groups // 2
    nc = seq // S5_Q
    pair_rows = 2 * S5_GROUP
    per_b = pl.BlockSpec((1, 1, d), lambda b, i, k: (b, 0, 0))

    xv = x.reshape(bsz, nc, S5_Q, d)
    tg = V7X_SUBLANES

    def tok_spec(ncb):
        return pl.BlockSpec((1, ncb, tg, d), lambda b, i, k: (b, i, k, 0))

    def chan_spec(ncb):
        return pl.BlockSpec((1, tg, d, ncb), lambda b, i, k: (b, k, 0, i))

    def lane_tiles(ncb):
        return [pltpu.VMEM((ncb * tg, V7X_LANES), F32)] * (d // V7X_LANES)

    ncb = S5_IN_CHUNKS
    ut = pl.pallas_call(
        _s5_in_kernel,
        grid=(bsz, nc // ncb, S5_Q // tg),
        in_specs=[tok_spec(ncb), _const_spec((1, d)), per_b, per_b, _const_spec((d, d))],
        out_specs=chan_spec(ncb),
        out_shape=jax.ShapeDtypeStruct((bsz, S5_Q, d, nc), BF16),
        scratch_shapes=lane_tiles(ncb),
        compiler_params=_compiler_params(("parallel", "parallel", "parallel"),
                                         _nbytes((ncb, tg, d), F32) + _nbytes((tg, d, ncb), BF16),
                                         _nbytes((d, d), BF16) + _nbytes((ncb, tg, d), F32)
                                         + 8 * _nbytes((ncb, d), F32)),
        name="s5_in",
    )(xv, norm_g.reshape(1, d), shift, scale, w_in.astype(BF16))

    toep, ws, wy, a_pair = _s5_tables(lam_re, lam_im, log_dt, b_re, b_im, c_re, c_im, d_skip)

    rows = bsz * pairs
    xspec = pl.BlockSpec((1, S5_Q, pair_rows, nc), lambda b, p: (b, 0, p, 0))
    sspec = pl.BlockSpec((nc, S5_PAIR_STATE), lambda b, p: (0, b * pairs + p))

    def table_spec(r, c):
        return pl.BlockSpec((2, r, c), lambda b, p: (p, 0, 0))

    s_loc = pl.pallas_call(
        _s5_state_kernel,
        grid=(bsz, pairs),
        in_specs=[xspec, table_spec(2 * S5_STATE, S5_QW)],
        out_specs=sspec,
        out_shape=jax.ShapeDtypeStruct((nc, rows * S5_PAIR_STATE), F32),
        compiler_params=_compiler_params(
            ("parallel", "parallel"),
            _nbytes((nc, 2 * S5_QW), BF16) + _nbytes((nc, S5_PAIR_STATE), F32),
            3 * _nbytes((nc, S5_PAIR_STATE), F32)),
        name="s5_chunk_state",
    )(ut, ws)

    cb = SCAN_BLOCK
    sblk = pl.BlockSpec((cb, rows, S5_PAIR_STATE), lambda i: (i, 0, 0))
    h_in = pl.pallas_call(
        _s5_scan_kernel,
        grid=(nc // cb,),
        in_specs=[_const_spec((rows, S5_PAIR_STATE)), sblk],
        out_specs=sblk,
        out_shape=jax.ShapeDtypeStruct((nc, rows, S5_PAIR_STATE), F32),
        scratch_shapes=[pltpu.VMEM((rows, S5_PAIR_STATE), F32)],
        compiler_params=_compiler_params(("arbitrary",),
                                         2 * _nbytes((cb, rows, S5_PAIR_STATE), F32),
                                         1024 * 1024),
        name="s5_chunk_scan",
    )(jnp.tile(a_pair, (bsz, 1)), s_loc.reshape(nc, rows, S5_PAIR_STATE))

    yt = pl.pallas_call(
        _s5_y_kernel,
        grid=(bsz, pairs),
        in_specs=[xspec, sspec, table_spec(S5_QW, S5_QW), table_spec(S5_QW, 2 * S5_STATE)],
        out_specs=xspec,
        out_shape=jax.ShapeDtypeStruct((bsz, S5_Q, d, nc), BF16),
        compiler_params=_compiler_params(
            ("parallel", "parallel"),
            _nbytes((nc, 2 * S5_QW), BF16) + _nbytes((nc, 2 * S5_QW), F32)
            + _nbytes((nc, S5_PAIR_STATE), F32),
            3 * _nbytes((nc, 2 * S5_QW), F32)),
        name="s5_chunk_output",
    )(ut, h_in.reshape(nc, rows * S5_PAIR_STATE), toep, wy)

    ncb = S5_OUT_CHUNKS
    out = pl.pallas_call(
        _s5_out_kernel,
        grid=(bsz, nc // ncb, S5_Q // tg),
        in_specs=[chan_spec(ncb), tok_spec(ncb), _const_spec((d, 2 * d)),
                  _const_spec((1, 2 * d)), per_b],
        out_specs=tok_spec(ncb),
        out_shape=jax.ShapeDtypeStruct(xv.shape, F32),
        scratch_shapes=lane_tiles(ncb),
        compiler_params=_compiler_params(
            ("parallel", "parallel", "parallel"),
            3 * _nbytes((ncb, tg, d), F32),
            _nbytes((d, 2 * d), BF16) + _nbytes((ncb, tg, d), F32)
            + 6 * _nbytes((S5_OUT_OFFSETS * ncb, 2 * d), F32)),
        name="s5_out",
    )(yt, xv, w_glu.astype(BF16), b_glu.reshape(1, 2 * d), gate)
    return out.reshape(bsz, seq, d)


def _ssd_kernel(x_ref, g_ref, sh_ref, sc_ref, gate_ref, win_ref, wdtt_ref, cw_ref, cb_ref,
                biast_ref, alogt_ref, dskip_ref, ng_ref, wo_ref, o_ref,
                h_ref, xr_ref, dtt_ref, y_ref, halo_ref, prev_ref, st_ref, *tiles, d_inner, heads):
    q = M2_CHUNK
    n = M2_STATE
    hp = M2_HEAD_PAIR
    ph_n = M2_PHASES
    rpc = M2_PHASE_ROWS
    kc = M2_BLOCK_CHUNKS
    halo = V7X_SUBLANES
    n_prev = M2_CONV - 1

    @pl.when(pl.program_id(1) == 0)
    def _():
        halo_ref[:, halo - 1:halo, :] = jnp.zeros((n_prev, 1, halo_ref.shape[2]), F32)
        st_ref[...] = jnp.zeros_like(st_ref)

    _stage_lane_tiles(x_ref, tiles)
    xb = jnp.concatenate([_strided_rows(tiles, k * q + ph, rpc, ph_n)
                          for k in range(kc) for ph in range(ph_n)], axis=0)
    h = _rms_mod(xb, g_ref[...], sh_ref[0], sc_ref[0]).astype(BF16)
    h_ref[...] = h
    conv_dim = xr_ref.shape[2]
    xr = jnp.dot(h, win_ref[:, d_inner:d_inner + conv_dim], preferred_element_type=F32)
    dtt = _softplus(lax.dot_general(wdtt_ref[...], h, (((1,), (1,)), ((), ())),
                                    preferred_element_type=F32) + biast_ref[...])
    for k in range(kc):
        dtt_ref[k] = dtt[:, k * q:(k + 1) * q]

    for k in range(kc):
        xr_ref[k] = xr[k * q:(k + 1) * q, :]
    for j in range(n_prev):
        ph = ph_n - n_prev + j
        for k in range(kc):
            halo_ref[j, halo + k * rpc:halo + (k + 1) * rpc, :] = (
                xr[k * q + ph * rpc:k * q + (ph + 1) * rpc, :])
    for j in range(n_prev):
        back = halo_ref[j, pl.ds(halo - 1, kc * rpc), :]
        for k in range(kc):
            prev_ref[k, j * rpc:(j + 1) * rpc, :] = back[k * rpc:(k + 1) * rpc]
        halo_ref[j, halo - 1:halo, :] = halo_ref[j, halo + kc * rpc - 1:halo + kc * rpc, :]

    shift_bits = rpc.bit_length() - 1
    row = lax.broadcasted_iota(jnp.int32, (q, q), 0)
    col = lax.broadcasted_iota(jnp.int32, (q, q), 1)
    pos_r = (row & (rpc - 1)) * ph_n + lax.shift_right_logical(row, shift_bits)
    pos_c = (col & (rpc - 1)) * ph_n + lax.shift_right_logical(col, shift_bits)
    causal = pos_r >= pos_c
    cum_mat = (pos_r <= pos_c).astype(F32)
    neg_a = -jnp.exp(alogt_ref[...])
    lane = lax.broadcasted_iota(jnp.int32, (q, hp), 1)
    first = lane < M2_HEADDIM
    first_row = lax.broadcasted_iota(jnp.int32, (1, hp), 1) < M2_HEADDIM
    heads_per_group = heads // M2_GROUPS
    pairs_per_group = heads_per_group // 2

    def core(k):
        xk = xr_ref[k]
        pk = prev_ref[k]
        acc = cb_ref[...] + cw_ref[M2_CONV - 1:M2_CONV, :] * xk
        for s in range(1, M2_CONV):
            shifted = jnp.concatenate([pk[(n_prev - s) * rpc:], xk[:q - s * rpc]], axis=0)
            acc += cw_ref[M2_CONV - 1 - s:M2_CONV - s, :] * shifted
        xbc = acc * _sigmoid(acc)
        dtk = dtt_ref[k]
        a_cst = LOG2_E * jnp.dot(dtk * neg_a, cum_mat, precision=HIGHEST,
                                 preferred_element_type=F32)
        src = a_cst - LOG2_E * jnp.log(dtk)
        a_cs = jnp.concatenate([a_cst, jnp.zeros((q - heads, q), F32)], axis=0).T
        state_w = jnp.exp2(a_cst[:, q - 1:q] - a_cst) * dtk
        chunk_decay = jnp.exp2(a_cs[q - 1:q, :])
        for g in range(M2_GROUPS):
            b_g = xbc[:, d_inner + g * n:d_inner + (g + 1) * n]
            c_g = xbc[:, d_inner + M2_GROUPS * n + g * n:d_inner + M2_GROUPS * n + (g + 1) * n]
            cb = lax.dot_general(c_g.astype(BF16), b_g.astype(BF16), (((1,), (1,)), ((), ())),
                                 preferred_element_type=F32)
            b_t = b_g.T
            for j in range(pairs_per_group):
                h0 = g * heads_per_group + 2 * j
                sl = slice((h0 // 2) * hp, (h0 // 2 + 1) * hp)
                lhs, lhs_state = [], []
                for hd in (h0, h0 + 1):
                    bc = jnp.broadcast_to(a_cs[:, hd:hd + 1], (q, q))
                    decay_dt = jnp.exp2(jnp.where(causal, bc - src[hd:hd + 1, :], -jnp.inf))
                    lhs.append(cb * decay_dt)
                    lhs_state.append(b_t * state_w[hd:hd + 1, :])
                for hd in (h0, h0 + 1):
                    lhs.append(c_g * jnp.exp2(jnp.broadcast_to(a_cs[:, hd:hd + 1], (q, n))))
                xs_p = xbc[:, sl]
                st_p = st_ref[:, sl]
                x0 = jnp.where(first, xs_p, 0.0)
                x1 = jnp.where(first, 0.0, xs_p)
                s0 = jnp.where(first, st_p, 0.0)
                s1 = jnp.where(first, 0.0, st_p)
                lhs = jnp.concatenate(lhs, axis=1).astype(BF16)
                rhs = jnp.concatenate([x0, x1, s0, s1], axis=0).astype(BF16)
                y = jnp.dot(lhs, rhs, preferred_element_type=F32)
                y_ref[k, :, sl] = y + xs_p * dskip_ref[:, sl]
                s_new = jnp.dot(jnp.concatenate(lhs_state, axis=1).astype(BF16),
                                jnp.concatenate([x0, x1], axis=0).astype(BF16),
                                preferred_element_type=F32)
                cd = jnp.where(first_row, chunk_decay[:, h0:h0 + 1], chunk_decay[:, h0 + 1:h0 + 2])
                st_ref[:, sl] = st_p * cd + s_new

    def post(k):
        z = jnp.dot(h_ref[pl.ds(k * q, q), :], win_ref[:, :d_inner],
                    preferred_element_type=F32)
        y = y_ref[k] * (z * _sigmoid(z))
        gw = d_inner // M2_GROUPS
        parts = []
        for g in range(M2_GROUPS):
            yg = y[:, g * gw:(g + 1) * gw]
            parts.append(yg * lax.rsqrt(jnp.mean(yg * yg, axis=-1, keepdims=True) + NORM_EPS))
        y = jnp.concatenate(parts, axis=1) * ng_ref[...]
        out = jnp.dot(y.astype(BF16), wo_ref[...], preferred_element_type=F32)
        for ph in range(ph_n):
            res = (_strided_rows(tiles, k * q + ph, rpc, ph_n)
                   + gate_ref[0] * out[ph * rpc:(ph + 1) * rpc])
            _set_strided_rows(tiles, k * q + ph, rpc, ph_n, res)

    core(0)

    def step(k, carry):
        post(k - 1)
        core(k)
        return carry

    lax.fori_loop(1, kc, step, 0)
    post(kc - 1)
    _unstage_lane_tiles(tiles, o_ref)


def _ssd_layer(x, norm_g, shift, scale, gate, w_in, conv_w, conv_b, dt_bias, a_log,
               d_skip, gn_g, w_out):
    bsz, seq, d = x.shape
    d_inner = w_out.shape[0]
    heads = d_inner // M2_HEADDIM
    conv_dim = d_inner + 2 * M2_GROUPS * M2_STATE
    q = M2_CHUNK
    kc = M2_BLOCK_CHUNKS
    rows = kc * M2_PHASE_ROWS

    w_dt_t = w_in[:, d_inner + conv_dim:].T.astype(BF16)

    xv = x.reshape(bsz, seq // M2_PHASES, M2_PHASES, d)
    blk = pl.BlockSpec((1, rows, M2_PHASES, d), lambda b, i: (b, i, 0, 0))
    per_b = pl.BlockSpec((1, 1, d), lambda b, i: (b, 0, 0))
    out = pl.pallas_call(
        functools.partial(_ssd_kernel, d_inner=d_inner, heads=heads),
        grid=(bsz, seq // (kc * q)),
        in_specs=[blk, _const_spec((1, d)), per_b, per_b, per_b,
                  _const_spec(w_in.shape), _const_spec((heads, d)),
                  _const_spec((M2_CONV, conv_dim)), _const_spec((1, conv_dim)),
                  _const_spec((heads, 1)), _const_spec((heads, 1)),
                  _const_spec((1, d_inner)), _const_spec((1, d_inner)), _const_spec((d_inner, d))],
        out_specs=blk,
        out_shape=jax.ShapeDtypeStruct(xv.shape, F32),
        scratch_shapes=[pltpu.VMEM((kc * q, d), BF16),
                        pltpu.VMEM((kc, q, conv_dim), F32),
                        pltpu.VMEM((kc, heads, q), F32),
                        pltpu.VMEM((kc, q, d_inner), F32),
                        pltpu.VMEM((M2_CONV - 1, V7X_SUBLANES + rows, conv_dim), F32),
                        pltpu.VMEM((kc, (M2_CONV - 1) * M2_PHASE_ROWS, conv_dim), F32),
                        pltpu.VMEM((M2_STATE, d_inner), F32)]
        + [pltpu.VMEM((kc * q, V7X_LANES), F32)] * (d // V7X_LANES),
        compiler_params=_compiler_params(
            ("arbitrary", "arbitrary"),
            2 * _nbytes((kc * q, d), F32),
            _nbytes((d, 2 * d_inner + conv_dim), BF16)
            + 2 * _nbytes((kc * q, conv_dim + d_inner), F32)
            + 2 * _nbytes((kc * q, conv_dim), F32)),
        name="ssd_mixer",
    )(xv, norm_g.reshape(1, d), shift, scale, gate, w_in.astype(BF16), w_dt_t,
      conv_w, conv_b.reshape(1, conv_dim), dt_bias.reshape(heads, 1), a_log.reshape(heads, 1),
      jnp.repeat(d_skip, M2_HEADDIM).reshape(1, d_inner), gn_g.reshape(1, d_inner),
      w_out.astype(BF16))
    return out.reshape(bsz, seq, d)


def kernel(x, c, ada_w, ada_b, norm_mix_g, norm_mlp_g, mlp_w1, mlp_w2, s5_w_in, s5_lambda_re, s5_lambda_im, s5_log_dt, s5_b_re, s5_b_im, s5_c_re, s5_c_im, s5_d, s5_w_glu, s5_b_glu, m2_w_in, m2_conv_w, m2_conv_b, m2_dt_bias, m2_a_log, m2_d, m2_norm_g, m2_w_out, final_norm_g):
    depth = ada_w.shape[0]
    bsz, _, d = x.shape
    mod = _ada_modulation(c, ada_w, ada_b).reshape(depth, bsz, N_MOD, 1, d)
    w1_all, w2_all = mlp_w1.astype(BF16), mlp_w2.astype(BF16)
    for i in range(depth):
        sh1, sc1, g1, sh2, sc2, g2 = (mod[i, :, k] for k in range(N_MOD))
        j = i // 2
        if i % 2 == 0:
            x = _s5_layer(x, norm_mix_g[i], sh1, sc1, g1, s5_w_in[j], s5_lambda_re[j],
                          s5_lambda_im[j], s5_log_dt[j], s5_b_re[j], s5_b_im[j],
                          s5_c_re[j], s5_c_im[j], s5_d[j], s5_w_glu[j], s5_b_glu[j])
        else:
            x = _ssd_layer(x, norm_mix_g[i], sh1, sc1, g1, m2_w_in[j], m2_conv_w[j],
                           m2_conv_b[j], m2_dt_bias[j], m2_a_log[j], m2_d[j],
                           m2_norm_g[j], m2_w_out[j])
        x = _mlp_layer(x, norm_mlp_g[i], sh2, sc2, g2, w1_all, w2_all, i,
                       final_g=final_norm_g if i == depth - 1 else None)
    return x
```

```python
import functools

import jax
import jax.numpy as jnp
from jax import lax
from jax.experimental import pallas as pl
from jax.experimental.pallas import tpu as pltpu

F32 = jnp.float32
BF16 = jnp.bfloat16
HIGHEST = lax.Precision.HIGHEST
NORM_EPS = 1e-5
N_MOD = 6
LOG2_E = 1.4426950408889634

V7X_LANES = 128
V7X_SUBLANES = 8
V7X_SCOPED_VMEM_BYTES = 60000 * 1024

S5_GROUP = 16
S5_STATE = 64
S5_Q = 16
S5_QW = S5_Q * S5_GROUP
S5_PAIR_STATE = 4 * S5_STATE
S5_IN_CHUNKS = 256
S5_OUT_CHUNKS = 128
S5_OUT_OFFSETS = 4

M2_HEADDIM = 64
M2_GROUPS = 4
M2_STATE = 128
M2_CONV = 4
M2_CHUNK = 128
M2_HEAD_PAIR = 2 * M2_HEADDIM
M2_PHASES = V7X_SUBLANES
M2_PHASE_ROWS = M2_CHUNK // M2_PHASES
M2_BLOCK_CHUNKS = 4

TOKEN_BLOCK = 512
SCAN_BLOCK = 64


def _compiler_params(semantics, block_bytes, temp_bytes):
    want = 2 * block_bytes + temp_bytes
    return pltpu.CompilerParams(
        dimension_semantics=semantics,
        vmem_limit_bytes=int(min(V7X_SCOPED_VMEM_BYTES, max(want, 16 * 1024 * 1024))))


def _nbytes(shape, dtype):
    n = 1
    for s in shape:
        n *= s
    return n * jnp.dtype(dtype).itemsize


def _const_spec(shape):
    return pl.BlockSpec(shape, lambda *_: (0,) * len(shape), pipeline_mode=pl.Buffered(1))


def _rms_mod(x, g, shift, scale):
    y = x * lax.rsqrt(jnp.mean(x * x, axis=-1, keepdims=True) + NORM_EPS)
    return (y * g) * (1.0 + scale) + shift


def _softplus(x):
    return jnp.maximum(x, 0.0) + jnp.log1p(jnp.exp(-jnp.abs(x)))


def _sigmoid(x):
    return 0.5 * jnp.tanh(0.5 * x) + 0.5


def _silu(x):
    h = 0.5 * x
    return h + h * jnp.tanh(h)


def _ada_kernel(c_ref, w_ref, b_ref, o_ref):
    cond = jax.nn.silu(c_ref[...])
    o_ref[0] = jnp.dot(cond, w_ref[0], precision=HIGHEST,
                       preferred_element_type=F32) + b_ref[0]


def _ada_modulation(c, ada_w, ada_b):
    depth, d, n = ada_w.shape
    bsz = c.shape[0]
    rows = V7X_SUBLANES
    nb = n // 4
    c_pad = jnp.zeros((rows, d), F32).at[:bsz].set(c)
    out = pl.pallas_call(
        _ada_kernel,
        grid=(depth, n // nb),
        in_specs=[pl.BlockSpec((rows, d), lambda i, j: (0, 0)),
                  pl.BlockSpec((1, d, nb), lambda i, j: (i, 0, j)),
                  pl.BlockSpec((1, 1, nb), lambda i, j: (i, 0, j))],
        out_specs=pl.BlockSpec((1, rows, nb), lambda i, j: (i, 0, j)),
        out_shape=jax.ShapeDtypeStruct((depth, rows, n), F32),
        compiler_params=_compiler_params(("arbitrary", "arbitrary"),
                                         _nbytes((d, nb), F32), 4 * 1024 * 1024),
        name="ada_modulation",
    )(c_pad, ada_w, ada_b.reshape(depth, 1, n))
    return out[:, :bsz, :]


def _mlp_kernel(x_ref, g_ref, sh_ref, sc_ref, gate_ref, w1_ref, w2_ref, *rest, final):
    if final:
        fg_ref, o_ref = rest
    else:
        (o_ref,) = rest
    x = x_ref[0]
    h = _rms_mod(x, g_ref[...], sh_ref[0], sc_ref[0])
    a = jnp.maximum(jnp.dot(h.astype(BF16), w1_ref[0], preferred_element_type=F32), 0.0)
    y = jnp.dot((a * a).astype(BF16), w2_ref[0], preferred_element_type=F32)
    out = x + gate_ref[0] * y
    if final:
        out = out * lax.rsqrt(jnp.mean(out * out, axis=-1, keepdims=True) + NORM_EPS)
        out = out * fg_ref[...]
    o_ref[0] = out


def _mlp_layer(x, norm_g, shift, scale, gate, w1_all, w2_all, layer, final_g=None):
    bsz, seq, d = x.shape
    f = w1_all.shape[2]
    tl = TOKEN_BLOCK
    tok = pl.BlockSpec((1, tl, d), lambda b, i: (b, i, 0))
    per_b = pl.BlockSpec((1, 1, d), lambda b, i: (b, 0, 0))

    def layer_spec(r, c):
        return pl.BlockSpec((1, r, c), lambda b, i: (layer, 0, 0), pipeline_mode=pl.Buffered(1))

    in_specs = [tok, _const_spec((1, d)), per_b, per_b, per_b, layer_spec(d, f), layer_spec(f, d)]
    args = [x, norm_g.reshape(1, d), shift, scale, gate, w1_all, w2_all]
    if final_g is not None:
        in_specs.append(_const_spec((1, d)))
        args.append(final_g.reshape(1, d))
    return pl.pallas_call(
        functools.partial(_mlp_kernel, final=final_g is not None),
        grid=(bsz, seq // tl),
        in_specs=in_specs,
        out_specs=tok,
        out_shape=jax.ShapeDtypeStruct(x.shape, F32),
        compiler_params=_compiler_params(
            ("parallel", "parallel"),
            2 * _nbytes((tl, d), F32) + _nbytes((d, f), BF16),
            _nbytes((tl, f), F32) * 2 + _nbytes((tl, d), F32) * 2),
        name="mlp",
    )(*args)


def _stage_lane_tiles(x_ref, tiles):
    r, s, _ = x_ref.shape[1:]
    for j, t_ref in enumerate(tiles):
        t_ref[...] = x_ref[0, :, :, j * V7X_LANES:(j + 1) * V7X_LANES].reshape(r * s, V7X_LANES)


def _unstage_lane_tiles(tiles, o_ref):
    r, s, _ = o_ref.shape[1:]
    for j, t_ref in enumerate(tiles):
        o_ref[0, :, :, j * V7X_LANES:(j + 1) * V7X_LANES] = t_ref[...].reshape(r, s, V7X_LANES)


def _strided_rows(tiles, start, size, stride):
    return jnp.concatenate([t[pl.ds(start, size, stride=stride), :] for t in tiles], axis=1)


def _set_strided_rows(tiles, start, size, stride, value):
    for j, t_ref in enumerate(tiles):
        t_ref[pl.ds(start, size, stride=stride), :] = value[:, j * V7X_LANES:(j + 1) * V7X_LANES]


def _s5_in_kernel(x_ref, g_ref, sh_ref, sc_ref, w_ref, u_ref, *tiles):
    nc, tg = x_ref.shape[1:3]
    _stage_lane_tiles(x_ref, tiles)
    for j in range(tg):
        h = _rms_mod(_strided_rows(tiles, j, nc, tg), g_ref[...], sh_ref[0], sc_ref[0])
        u = jnp.dot(h.astype(BF16), w_ref[...], preferred_element_type=F32)
        u_ref[0, j] = u.T.astype(BF16)


def _group_inputs(x_ref, gi):
    q, _, nc = x_ref.shape[1:]
    return x_ref[0, :, gi * S5_GROUP:(gi + 1) * S5_GROUP, :].reshape(q * S5_GROUP, nc)


def _s5_state_kernel(x_ref, ws_ref, s_ref):
    p = S5_STATE
    parts = [jnp.dot(ws_ref[gi], _group_inputs(x_ref, gi), preferred_element_type=F32)
             for gi in range(2)]
    st = jnp.concatenate([parts[0][:p], parts[1][:p], parts[0][p:], parts[1][p:]], axis=0)
    s_ref[...] = st.T


def _s5_scan_kernel(a_ref, s_ref, o_ref, h_ref):
    half = a_ref.shape[1] // 2

    @pl.when(pl.program_id(0) == 0)
    def _():
        h_ref[...] = jnp.zeros_like(h_ref)

    a_re = a_ref[:, :half]
    a_im = a_ref[:, half:]

    def body(c, carry):
        h_re, h_im = carry
        o_ref[c, :, :half] = h_re
        o_ref[c, :, half:] = h_im
        s = s_ref[c]
        return (a_re * h_re - a_im * h_im + s[:, :half],
                a_re * h_im + a_im * h_re + s[:, half:])

    h_re, h_im = lax.fori_loop(0, s_ref.shape[0], body,
                               (h_ref[:, :half], h_ref[:, half:]))
    h_ref[:, :half] = h_re
    h_ref[:, half:] = h_im


def _s5_y_kernel(x_ref, h_ref, t_ref, wy_ref, y_ref):
    q = x_ref.shape[1]
    nc = x_ref.shape[3]
    p = S5_STATE
    ht = h_ref[...].T.astype(BF16)
    for gi in range(2):
        hg = jnp.concatenate([ht[gi * p:(gi + 1) * p], ht[(2 + gi) * p:(3 + gi) * p]], axis=0)
        y = jnp.dot(t_ref[gi], _group_inputs(x_ref, gi), preferred_element_type=F32)
        y += jnp.dot(wy_ref[gi], hg, preferred_element_type=F32)
        y_ref[0, :, gi * S5_GROUP:(gi + 1) * S5_GROUP, :] = y.reshape(q, S5_GROUP, nc)


def _s5_out_kernel(y_ref, x_ref, w_ref, b_ref, gate_ref, o_ref, *tiles):
    d = w_ref.shape[0]
    nt, nc = y_ref.shape[1], y_ref.shape[3]
    tg = S5_OUT_OFFSETS
    _stage_lane_tiles(x_ref, tiles)
    for j0 in range(0, nt, tg):
        g = jnp.concatenate([jax.nn.gelu(y_ref[0, j0 + j]).T for j in range(tg)], axis=0)
        ab = jnp.dot(g.astype(BF16), w_ref[...], preferred_element_type=F32) + b_ref[...]
        mix = ab[:, :d] * _sigmoid(ab[:, d:])
        for j in range(tg):
            out = (_strided_rows(tiles, j0 + j, nc, nt)
                   + gate_ref[0] * mix[j * nc:(j + 1) * nc])
            _set_strided_rows(tiles, j0 + j, nc, nt, out)
    _unstage_lane_tiles(tiles, o_ref)


def _s5_tables(lam_re, lam_im, log_dt, b_re, b_im, c_re, c_im, d_skip):
    g, p = lam_re.shape
    q, hh = S5_Q, S5_GROUP
    dt = jnp.exp(log_dt)[:, None]
    ld_re, ld_im = lam_re * dt, lam_im * dt

    def lam_pow(k):
        mag = jnp.exp(ld_re[..., None] * k)
        ang = ld_im[..., None] * k
        return mag * jnp.cos(ang), mag * jnp.sin(ang)

    pw_re, pw_im = lam_pow(jnp.arange(q + 1, dtype=F32))
    lb_re, lb_im = pw_re[..., 1], pw_im[..., 1]
    den = lam_re * lam_re + lam_im * lam_im
    f_re = ((lb_re - 1.0) * lam_re + lb_im * lam_im) / den
    f_im = (lb_im * lam_re - (lb_re - 1.0) * lam_im) / den
    bb_re = f_re[..., None] * b_re - f_im[..., None] * b_im
    bb_im = f_re[..., None] * b_im + f_im[..., None] * b_re

    kp_re = jnp.transpose(pw_re[..., :q], (0, 2, 1))[:, :, None, :]
    kp_im = jnp.transpose(pw_im[..., :q], (0, 2, 1))[:, :, None, :]
    cp_re = c_re[:, None] * kp_re - c_im[:, None] * kp_im
    cp_im = c_re[:, None] * kp_im + c_im[:, None] * kp_re
    lag = (jnp.einsum('gkop,gpi->gkoi', cp_re, bb_re, precision=HIGHEST)
           - jnp.einsum('gkop,gpi->gkoi', cp_im, bb_im, precision=HIGHEST))
    lag = lag.at[:, 0].add(d_skip.reshape(g, hh)[:, :, None] * jnp.eye(hh, dtype=F32))
    tt = jnp.arange(q)
    sub_diag = (tt[None, :, None] - tt[None, None, :] == tt[:, None, None]).astype(F32)
    toep = jnp.einsum('kut,gkoh->guoth', sub_diag, lag, precision=HIGHEST).reshape(g, q * hh, q * hh)

    rv_re = jnp.repeat(pw_re[..., :q][..., ::-1], hh, axis=-1)
    rv_im = jnp.repeat(pw_im[..., :q][..., ::-1], hh, axis=-1)
    bt_re, bt_im = jnp.tile(bb_re, (1, 1, q)), jnp.tile(bb_im, (1, 1, q))
    ws = jnp.concatenate([rv_re * bt_re - rv_im * bt_im, rv_re * bt_im + rv_im * bt_re], axis=1)

    up_re = jnp.transpose(pw_re[..., 1:], (0, 2, 1))[:, :, None, :]
    up_im = jnp.transpose(pw_im[..., 1:], (0, 2, 1))[:, :, None, :]
    cy_re = (c_re[:, None] * up_re - c_im[:, None] * up_im).reshape(g, q * hh, p)
    cy_im = (c_re[:, None] * up_im + c_im[:, None] * up_re).reshape(g, q * hh, p)
    wy = jnp.concatenate([cy_re, -cy_im], axis=2)

    a_p = jnp.concatenate([pw_re[..., q].reshape(g // 2, 2 * p),
                           pw_im[..., q].reshape(g // 2, 2 * p)], axis=1)
    return toep.astype(BF16), ws.astype(BF16), wy.astype(BF16), a_p


def _s5_layer(x, norm_g, shift, scale, gate, w_in, lam_re, lam_im, log_dt,
              b_re, b_im, c_re, c_im, d_skip, w_glu, b_glu):
    bsz, seq, d = x.shape
    groups = d // S5_GROUP
    pairs = groups // 2
    nc = seq // S5_Q
    pair_rows = 2 * S5_GROUP
    per_b = pl.BlockSpec((1, 1, d), lambda b, i, k: (b, 0, 0))

    xv = x.reshape(bsz, nc, S5_Q, d)
    tg = V7X_SUBLANES

    def tok_spec(ncb):
        return pl.BlockSpec((1, ncb, tg, d), lambda b, i, k: (b, i, k, 0))

    def chan_spec(ncb):
        return pl.BlockSpec((1, tg, d, ncb), lambda b, i, k: (b, k, 0, i))

    def lane_tiles(ncb):
        return [pltpu.VMEM((ncb * tg, V7X_LANES), F32)] * (d // V7X_LANES)

    ncb = S5_IN_CHUNKS
    ut = pl.pallas_call(
        _s5_in_kernel,
        grid=(bsz, nc // ncb, S5_Q // tg),
        in_specs=[tok_spec(ncb), _const_spec((1, d)), per_b, per_b, _const_spec((d, d))],
        out_specs=chan_spec(ncb),
        out_shape=jax.ShapeDtypeStruct((bsz, S5_Q, d, nc), BF16),
        scratch_shapes=lane_tiles(ncb),
        compiler_params=_compiler_params(("parallel", "parallel", "parallel"),
                                         _nbytes((ncb, tg, d), F32) + _nbytes((tg, d, ncb), BF16),
                                         _nbytes((d, d), BF16) + _nbytes((ncb, tg, d), F32)
                                         + 8 * _nbytes((ncb, d), F32)),
        name="s5_in",
    )(xv, norm_g.reshape(1, d), shift, scale, w_in.astype(BF16))

    toep, ws, wy, a_pair = _s5_tables(lam_re, lam_im, log_dt, b_re, b_im, c_re, c_im, d_skip)

    rows = bsz * pairs
    xspec = pl.BlockSpec((1, S5_Q, pair_rows, nc), lambda b, p: (b, 0, p, 0))
    sspec = pl.BlockSpec((nc, S5_PAIR_STATE), lambda b, p: (0, b * pairs + p))

    def table_spec(r, c):
        return pl.BlockSpec((2, r, c), lambda b, p: (p, 0, 0))

    s_loc = pl.pallas_call(
        _s5_state_kernel,
        grid=(bsz, pairs),
        in_specs=[xspec, table_spec(2 * S5_STATE, S5_QW)],
        out_specs=sspec,
        out_shape=jax.ShapeDtypeStruct((nc, rows * S5_PAIR_STATE), F32),
        compiler_params=_compiler_params(
            ("parallel", "parallel"),
            _nbytes((nc, 2 * S5_QW), BF16) + _nbytes((nc, S5_PAIR_STATE), F32),
            3 * _nbytes((nc, S5_PAIR_STATE), F32)),
        name="s5_chunk_state",
    )(ut, ws)

    cb = SCAN_BLOCK
    sblk = pl.BlockSpec((cb, rows, S5_PAIR_STATE), lambda i: (i, 0, 0))
    h_in = pl.pallas_call(
        _s5_scan_kernel,
        grid=(nc // cb,),
        in_specs=[_const_spec((rows, S5_PAIR_STATE)), sblk],
        out_specs=sblk,
        out_shape=jax.ShapeDtypeStruct((nc, rows, S5_PAIR_STATE), F32),
        scratch_shapes=[pltpu.VMEM((rows, S5_PAIR_STATE), F32)],
        compiler_params=_compiler_params(("arbitrary",),
                                         2 * _nbytes((cb, rows, S5_PAIR_STATE), F32),
                                         1024 * 1024),
        name="s5_chunk_scan",
    )(jnp.tile(a_pair, (bsz, 1)), s_loc.reshape(nc, rows, S5_PAIR_STATE))

    yt = pl.pallas_call(
        _s5_y_kernel,
        grid=(bsz, pairs),
        in_specs=[xspec, sspec, table_spec(S5_QW, S5_QW), table_spec(S5_QW, 2 * S5_STATE)],
        out_specs=xspec,
        out_shape=jax.ShapeDtypeStruct((bsz, S5_Q, d, nc), F32),
        compiler_params=_compiler_params(
            ("parallel", "parallel"),
            _nbytes((nc, 2 * S5_QW), BF16) + _nbytes((nc, 2 * S5_QW), F32)
            + _nbytes((nc, S5_PAIR_STATE), F32),
            3 * _nbytes((nc, 2 * S5_QW), F32)),
        name="s5_chunk_output",
    )(ut, h_in.reshape(nc, rows * S5_PAIR_STATE), toep, wy)

    ncb = S5_OUT_CHUNKS
    out = pl.pallas_call(
        _s5_out_kernel,
        grid=(bsz, nc // ncb, S5_Q // tg),
        in_specs=[chan_spec(ncb), tok_spec(ncb), _const_spec((d, 2 * d)),
                  _const_spec((1, 2 * d)), per_b],
        out_specs=tok_spec(ncb),
        out_shape=jax.ShapeDtypeStruct(xv.shape, F32),
        scratch_shapes=lane_tiles(ncb),
        compiler_params=_compiler_params(
            ("parallel", "parallel", "parallel"),
            3 * _nbytes((ncb, tg, d), F32),
            _nbytes((d, 2 * d), BF16) + _nbytes((ncb, tg, d), F32)
            + 6 * _nbytes((S5_OUT_OFFSETS * ncb, 2 * d), F32)),
        name="s5_out",
    )(yt, xv, w_glu.astype(BF16), b_glu.reshape(1, 2 * d), gate)
    return out.reshape(bsz, seq, d)


def _ssd_kernel(x_ref, g_ref, sh_ref, sc_ref, gate_ref, win_ref, wdtt_ref, cw_ref, cb_ref,
                biast_ref, alogt_ref, dskip_ref, ng_ref, wo_ref, o_ref,
                h_ref, y_ref, halo_ref, st_ref, *tiles, d_inner, heads):
    q = M2_CHUNK
    n = M2_STATE
    hp = M2_HEAD_PAIR
    ph_n = M2_PHASES
    rpc = M2_PHASE_ROWS
    kc = M2_BLOCK_CHUNKS
    halo = V7X_SUBLANES
    n_prev = M2_CONV - 1

    @pl.when(pl.program_id(1) == 0)
    def _():
        halo_ref[:, halo - 1:halo, :] = jnp.zeros((n_prev, 1, halo_ref.shape[2]), F32)
        st_ref[...] = jnp.zeros_like(st_ref)

    _stage_lane_tiles(x_ref, tiles)
    xb = jnp.concatenate([_strided_rows(tiles, k * q + ph, rpc, ph_n)
                          for k in range(kc) for ph in range(ph_n)], axis=0)
    h_ref[...] = _rms_mod(xb, g_ref[...], sh_ref[0], sc_ref[0]).astype(BF16)
    conv_dim = halo_ref.shape[2]

    shift_bits = rpc.bit_length() - 1
    row = lax.broadcasted_iota(jnp.int32, (q, q), 0)
    col = lax.broadcasted_iota(jnp.int32, (q, q), 1)
    pos_r = (row & (rpc - 1)) * ph_n + lax.shift_right_logical(row, shift_bits)
    pos_c = (col & (rpc - 1)) * ph_n + lax.shift_right_logical(col, shift_bits)
    causal = pos_r >= pos_c
    cum_mat = (pos_r <= pos_c).astype(F32)
    neg_a = -jnp.exp(alogt_ref[...])
    lane = lax.broadcasted_iota(jnp.int32, (q, hp), 1)
    first = lane < M2_HEADDIM
    first_row = lax.broadcasted_iota(jnp.int32, (1, hp), 1) < M2_HEADDIM
    heads_per_group = heads // M2_GROUPS
    pairs_per_group = heads_per_group // 2

    def core(k):
        hk = h_ref[pl.ds(k * q, q), :]
        xk = jnp.dot(hk, win_ref[:, d_inner:d_inner + conv_dim], preferred_element_type=F32)
        dtk = _softplus(lax.dot_general(wdtt_ref[...], hk, (((1,), (1,)), ((), ())),
                                        preferred_element_type=F32) + biast_ref[...])
        prev = []
        for j in range(n_prev):
            ph = ph_n - n_prev + j
            halo_ref[j, halo:halo + rpc, :] = xk[ph * rpc:(ph + 1) * rpc, :]
            prev.append(halo_ref[j, pl.ds(halo - 1, rpc), :])
            halo_ref[j, halo - 1:halo, :] = halo_ref[j, halo + rpc - 1:halo + rpc, :]
        pk = jnp.concatenate(prev, axis=0)
        acc = cb_ref[...] + cw_ref[M2_CONV - 1:M2_CONV, :] * xk
        for s in range(1, M2_CONV):
            shifted = jnp.concatenate([pk[(n_prev - s) * rpc:], xk[:q - s * rpc]], axis=0)
            acc += cw_ref[M2_CONV - 1 - s:M2_CONV - s, :] * shifted
        xbc = _silu(acc)
        a_cst = LOG2_E * jnp.dot(dtk * neg_a, cum_mat, precision=HIGHEST,
                                 preferred_element_type=F32)
        src = a_cst - LOG2_E * jnp.log(dtk)
        a_cs = jnp.concatenate([a_cst, jnp.zeros((q - heads, q), F32)], axis=0).T
        state_w = jnp.exp2(a_cst[:, q - 1:q] - a_cst) * dtk
        chunk_decay = jnp.exp2(a_cs[q - 1:q, :])
        for g in range(M2_GROUPS):
            b_g = xbc[:, d_inner + g * n:d_inner + (g + 1) * n]
            c_g = xbc[:, d_inner + M2_GROUPS * n + g * n:d_inner + M2_GROUPS * n + (g + 1) * n]
            cb = lax.dot_general(c_g.astype(BF16), b_g.astype(BF16), (((1,), (1,)), ((), ())),
                                 preferred_element_type=F32)
            b_t = b_g.T
            for j in range(pairs_per_group):
                h0 = g * heads_per_group + 2 * j
                sl = slice((h0 // 2) * hp, (h0 // 2 + 1) * hp)
                lhs, lhs_state = [], []
                for hd in (h0, h0 + 1):
                    bc = jnp.broadcast_to(a_cs[:, hd:hd + 1], (q, q))
                    decay_dt = jnp.exp2(jnp.where(causal, bc - src[hd:hd + 1, :], -jnp.inf))
                    lhs.append(cb * decay_dt)
                    lhs_state.append(b_t * state_w[hd:hd + 1, :])
                for hd in (h0, h0 + 1):
                    lhs.append(c_g * jnp.exp2(jnp.broadcast_to(a_cs[:, hd:hd + 1], (q, n))))
                xs_p = xbc[:, sl]
                st_p = st_ref[:, sl]
                x0 = jnp.where(first, xs_p, 0.0)
                x1 = jnp.where(first, 0.0, xs_p)
                s0 = jnp.where(first, st_p, 0.0)
                s1 = jnp.where(first, 0.0, st_p)
                lhs = jnp.concatenate(lhs, axis=1).astype(BF16)
                rhs = jnp.concatenate([x0, x1, s0, s1], axis=0).astype(BF16)
                y = jnp.dot(lhs, rhs, preferred_element_type=F32)
                y_ref[k, :, sl] = y + xs_p * dskip_ref[:, sl]
                s_new = jnp.dot(jnp.concatenate(lhs_state, axis=1).astype(BF16),
                                jnp.concatenate([x0, x1], axis=0).astype(BF16),
                                preferred_element_type=F32)
                cd = jnp.where(first_row, chunk_decay[:, h0:h0 + 1], chunk_decay[:, h0 + 1:h0 + 2])
                st_ref[:, sl] = st_p * cd + s_new

    def post(k):
        z = jnp.dot(h_ref[pl.ds(k * q, q), :], win_ref[:, :d_inner],
                    preferred_element_type=F32)
        y = y_ref[k] * _silu(z)
        gw = d_inner // M2_GROUPS
        parts = []
        for g in range(M2_GROUPS):
            yg = y[:, g * gw:(g + 1) * gw]
            parts.append(yg * lax.rsqrt(jnp.mean(yg * yg, axis=-1, keepdims=True) + NORM_EPS))
        y = jnp.concatenate(parts, axis=1) * ng_ref[...]
        out = jnp.dot(y.astype(BF16), wo_ref[...], preferred_element_type=F32)
        for ph in range(ph_n):
            res = (_strided_rows(tiles, k * q + ph, rpc, ph_n)
                   + gate_ref[0] * out[ph * rpc:(ph + 1) * rpc])
            _set_strided_rows(tiles, k * q + ph, rpc, ph_n, res)

    core(0)

    def step(k, carry):
        post(k - 1)
        core(k)
        return carry

    lax.fori_loop(1, kc, step, 0)
    post(kc - 1)
    _unstage_lane_tiles(tiles, o_ref)


def _ssd_layer(x, norm_g, shift, scale, gate, w_in, conv_w, conv_b, dt_bias, a_log,
               d_skip, gn_g, w_out):
    bsz, seq, d = x.shape
    d_inner = w_out.shape[0]
    heads = d_inner // M2_HEADDIM
    conv_dim = d_inner + 2 * M2_GROUPS * M2_STATE
    q = M2_CHUNK
    kc = M2_BLOCK_CHUNKS
    rows = kc * M2_PHASE_ROWS

    w_dt_t = w_in[:, d_inner + conv_dim:].T.astype(BF16)

    xv = x.reshape(bsz, seq // M2_PHASES, M2_PHASES, d)
    blk = pl.BlockSpec((1, rows, M2_PHASES, d), lambda b, i: (b, i, 0, 0))
    per_b = pl.BlockSpec((1, 1, d), lambda b, i: (b, 0, 0))
    out = pl.pallas_call(
        functools.partial(_ssd_kernel, d_inner=d_inner, heads=heads),
        grid=(bsz, seq // (kc * q)),
        in_specs=[blk, _const_spec((1, d)), per_b, per_b, per_b,
                  _const_spec(w_in.shape), _const_spec((heads, d)),
                  _const_spec((M2_CONV, conv_dim)), _const_spec((1, conv_dim)),
                  _const_spec((heads, 1)), _const_spec((heads, 1)),
                  _const_spec((1, d_inner)), _const_spec((1, d_inner)), _const_spec((d_inner, d))],
        out_specs=blk,
        out_shape=jax.ShapeDtypeStruct(xv.shape, F32),
        scratch_shapes=[pltpu.VMEM((kc * q, d), BF16),
                        pltpu.VMEM((kc, q, d_inner), F32),
                        pltpu.VMEM((M2_CONV - 1, V7X_SUBLANES + M2_PHASE_ROWS, conv_dim), F32),
                        pltpu.VMEM((M2_STATE, d_inner), F32)]
        + [pltpu.VMEM((kc * q, V7X_LANES), F32)] * (d // V7X_LANES),
        compiler_params=_compiler_params(
            ("arbitrary", "arbitrary"),
            2 * _nbytes((kc * q, d), F32),
            _nbytes((d, 2 * d_inner + conv_dim), BF16)
            + 2 * _nbytes((kc * q, conv_dim + d_inner), F32)
            + 2 * _nbytes((kc * q, conv_dim), F32)),
        name="ssd_mixer",
    )(xv, norm_g.reshape(1, d), shift, scale, gate, w_in.astype(BF16), w_dt_t,
      conv_w, conv_b.reshape(1, conv_dim), dt_bias.reshape(heads, 1), a_log.reshape(heads, 1),
      jnp.repeat(d_skip, M2_HEADDIM).reshape(1, d_inner), gn_g.reshape(1, d_inner),
      w_out.astype(BF16))
    return out.reshape(bsz, seq, d)


def kernel(x, c, ada_w, ada_b, norm_mix_g, norm_mlp_g, mlp_w1, mlp_w2, s5_w_in, s5_lambda_re, s5_lambda_im, s5_log_dt, s5_b_re, s5_b_im, s5_c_re, s5_c_im, s5_d, s5_w_glu, s5_b_glu, m2_w_in, m2_conv_w, m2_conv_b, m2_dt_bias, m2_a_log, m2_d, m2_norm_g, m2_w_out, final_norm_g):
    depth = ada_w.shape[0]
    bsz, _, d = x.shape
    mod = _ada_modulation(c, ada_w, ada_b).reshape(depth, bsz, N_MOD, 1, d)
    w1_all, w2_all = mlp_w1.astype(BF16), mlp_w2.astype(BF16)
    for i in range(depth):
        sh1, sc1, g1, sh2, sc2, g2 = (mod[i, :, k] for k in range(N_MOD))
        j = i // 2
        if i % 2 == 0:
            x = _s5_layer(x, norm_mix_g[i], sh1, sc1, g1, s5_w_in[j], s5_lambda_re[j],
                          s5_lambda_im[j], s5_log_dt[j], s5_b_re[j], s5_b_im[j],
                          s5_c_re[j], s5_c_im[j], s5_d[j], s5_w_glu[j], s5_b_glu[j])
        else:
            x = _ssd_layer(x, norm_mix_g[i], sh1, sc1, g1, m2_w_in[j], m2_conv_w[j],
                           m2_conv_b[j], m2_dt_bias[j], m2_a_log[j], m2_d[j],
                           m2_norm_g[j], m2_w_out[j])
        x = _mlp_layer(x, norm_mlp_g[i], sh2, sc2, g2, w1_all, w2_all, i,
                       final_g=final_norm_g if i == depth - 1 else None)
    return x
```

```python
import functools

import jax
import jax.numpy as jnp
from jax import lax
from jax.experimental import pallas as pl
from jax.experimental.pallas import tpu as pltpu

F32 = jnp.float32
BF16 = jnp.bfloat16
HIGHEST = lax.Precision.HIGHEST
NORM_EPS = 1e-5
N_MOD = 6
LOG2_E = 1.4426950408889634

V7X_LANES = 128
V7X_SUBLANES = 8
V7X_SCOPED_VMEM_BYTES = 60000 * 1024

S5_GROUP = 16
S5_STATE = 64
S5_Q = 16
S5_QW = S5_Q * S5_GROUP
S5_PAIR_STATE = 4 * S5_STATE
S5_IN_CHUNKS = 256
S5_OUT_CHUNKS = 128
S5_OUT_OFFSETS = 4

M2_HEADDIM = 64
M2_GROUPS = 4
M2_STATE = 128
M2_CONV = 4
M2_CHUNK = 128
M2_HEAD_PAIR = 2 * M2_HEADDIM
M2_PHASES = V7X_SUBLANES
M2_PHASE_ROWS = M2_CHUNK // M2_PHASES
M2_BLOCK_CHUNKS = 4

TOKEN_BLOCK = 512
SCAN_BLOCK = 64


def _compiler_params(semantics, block_bytes, temp_bytes):
    want = 2 * block_bytes + temp_bytes
    return pltpu.CompilerParams(
        dimension_semantics=semantics,
        vmem_limit_bytes=int(min(V7X_SCOPED_VMEM_BYTES, max(want, 16 * 1024 * 1024))))


def _nbytes(shape, dtype):
    n = 1
    for s in shape:
        n *= s
    return n * jnp.dtype(dtype).itemsize


def _const_spec(shape):
    return pl.BlockSpec(shape, lambda *_: (0,) * len(shape), pipeline_mode=pl.Buffered(1))


def _rms_mod(x, g, shift, scale):
    y = x * lax.rsqrt(jnp.mean(x * x, axis=-1, keepdims=True) + NORM_EPS)
    return (y * g) * (1.0 + scale) + shift


def _softplus(x):
    return jnp.maximum(x, 0.0) + jnp.log1p(jnp.exp(-jnp.abs(x)))


def _sigmoid(x):
    return 0.5 * jnp.tanh(0.5 * x) + 0.5


def _ada_kernel(c_ref, w_ref, b_ref, o_ref):
    cond = jax.nn.silu(c_ref[...])
    o_ref[0] = jnp.dot(cond, w_ref[0], precision=HIGHEST,
                       preferred_element_type=F32) + b_ref[0]


def _ada_modulation(c, ada_w, ada_b):
    depth, d, n = ada_w.shape
    bsz = c.shape[0]
    rows = V7X_SUBLANES
    nb = n // 4
    c_pad = jnp.zeros((rows, d), F32).at[:bsz].set(c)
    out = pl.pallas_call(
        _ada_kernel,
        grid=(depth, n // nb),
        in_specs=[pl.BlockSpec((rows, d), lambda i, j: (0, 0)),
                  pl.BlockSpec((1, d, nb), lambda i, j: (i, 0, j)),
                  pl.BlockSpec((1, 1, nb), lambda i, j: (i, 0, j))],
        out_specs=pl.BlockSpec((1, rows, nb), lambda i, j: (i, 0, j)),
        out_shape=jax.ShapeDtypeStruct((depth, rows, n), F32),
        compiler_params=_compiler_params(("arbitrary", "arbitrary"),
                                         _nbytes((d, nb), F32), 4 * 1024 * 1024),
        name="ada_modulation",
    )(c_pad, ada_w, ada_b.reshape(depth, 1, n))
    return out[:, :bsz, :]


def _mlp_kernel(x_ref, g_ref, sh_ref, sc_ref, gate_ref, w1_ref, w2_ref, *rest, final):
    if final:
        fg_ref, o_ref = rest
    else:
        (o_ref,) = rest
    x = x_ref[0]
    h = _rms_mod(x, g_ref[...], sh_ref[0], sc_ref[0])
    a = jnp.maximum(jnp.dot(h.astype(BF16), w1_ref[0], preferred_element_type=F32), 0.0)
    y = jnp.dot((a * a).astype(BF16), w2_ref[0], preferred_element_type=F32)
    out = x + gate_ref[0] * y
    if final:
        out = out * lax.rsqrt(jnp.mean(out * out, axis=-1, keepdims=True) + NORM_EPS)
        out = out * fg_ref[...]
    o_ref[0] = out


def _mlp_layer(x, norm_g, shift, scale, gate, w1_all, w2_all, layer, final_g=None):
    bsz, seq, d = x.shape
    f = w1_all.shape[2]
    tl = TOKEN_BLOCK
    tok = pl.BlockSpec((1, tl, d), lambda b, i: (b, i, 0))
    per_b = pl.BlockSpec((1, 1, d), lambda b, i: (b, 0, 0))

    def layer_spec(r, c):
        return pl.BlockSpec((1, r, c), lambda b, i: (layer, 0, 0), pipeline_mode=pl.Buffered(1))

    in_specs = [tok, _const_spec((1, d)), per_b, per_b, per_b, layer_spec(d, f), layer_spec(f, d)]
    args = [x, norm_g.reshape(1, d), shift, scale, gate, w1_all, w2_all]
    if final_g is not None:
        in_specs.append(_const_spec((1, d)))
        args.append(final_g.reshape(1, d))
    return pl.pallas_call(
        functools.partial(_mlp_kernel, final=final_g is not None),
        grid=(bsz, seq // tl),
        in_specs=in_specs,
        out_specs=tok,
        out_shape=jax.ShapeDtypeStruct(x.shape, F32),
        compiler_params=_compiler_params(
            ("parallel", "parallel"),
            2 * _nbytes((tl, d), F32) + _nbytes((d, f), BF16),
            _nbytes((tl, f), F32) * 2 + _nbytes((tl, d), F32) * 2),
        name="mlp",
    )(*args)


def _stage_lane_tiles(x_ref, tiles):
    r, s, _ = x_ref.shape[1:]
    for j, t_ref in enumerate(tiles):
        t_ref[...] = x_ref[0, :, :, j * V7X_LANES:(j + 1) * V7X_LANES].reshape(r * s, V7X_LANES)


def _unstage_lane_tiles(tiles, o_ref):
    r, s, _ = o_ref.shape[1:]
    for j, t_ref in enumerate(tiles):
        o_ref[0, :, :, j * V7X_LANES:(j + 1) * V7X_LANES] = t_ref[...].reshape(r, s, V7X_LANES)


def _strided_rows(tiles, start, size, stride):
    return jnp.concatenate([t[pl.ds(start, size, stride=stride), :] for t in tiles], axis=1)


def _set_strided_rows(tiles, start, size, stride, value):
    for j, t_ref in enumerate(tiles):
        t_ref[pl.ds(start, size, stride=stride), :] = value[:, j * V7X_LANES:(j + 1) * V7X_LANES]


def _s5_in_kernel(x_ref, g_ref, sh_ref, sc_ref, w_ref, u_ref, *tiles):
    nc, tg = x_ref.shape[1:3]
    _stage_lane_tiles(x_ref, tiles)
    for j in range(tg):
        h = _rms_mod(_strided_rows(tiles, j, nc, tg), g_ref[...], sh_ref[0], sc_ref[0])
        u = jnp.dot(h.astype(BF16), w_ref[...], preferred_element_type=F32)
        u_ref[0, j] = u.T.astype(BF16)


def _group_inputs(x_ref, gi):
    q, _, nc = x_ref.shape[1:]
    return x_ref[0, :, gi * S5_GROUP:(gi + 1) * S5_GROUP, :].reshape(q * S5_GROUP, nc)


def _s5_state_kernel(x_ref, ws_ref, s_ref):
    p = S5_STATE
    parts = [jnp.dot(ws_ref[gi], _group_inputs(x_ref, gi), preferred_element_type=F32)
             for gi in range(2)]
    st = jnp.concatenate([parts[0][:p], parts[1][:p], parts[0][p:], parts[1][p:]], axis=0)
    s_ref[...] = st.T


def _s5_scan_kernel(a_ref, s_ref, o_ref, h_ref):
    half = a_ref.shape[1] // 2

    @pl.when(pl.program_id(0) == 0)
    def _():
        h_ref[...] = jnp.zeros_like(h_ref)

    a_re = a_ref[:, :half]
    a_im = a_ref[:, half:]

    def body(c, carry):
        h_re, h_im = carry
        o_ref[c, :, :half] = h_re
        o_ref[c, :, half:] = h_im
        s = s_ref[c]
        return (a_re * h_re - a_im * h_im + s[:, :half],
                a_re * h_im + a_im * h_re + s[:, half:])

    h_re, h_im = lax.fori_loop(0, s_ref.shape[0], body,
                               (h_ref[:, :half], h_ref[:, half:]))
    h_ref[:, :half] = h_re
    h_ref[:, half:] = h_im


def _s5_y_kernel(x_ref, h_ref, t_ref, wy_ref, y_ref):
    q = x_ref.shape[1]
    nc = x_ref.shape[3]
    p = S5_STATE
    ht = h_ref[...].T.astype(BF16)
    for gi in range(2):
        hg = jnp.concatenate([ht[gi * p:(gi + 1) * p], ht[(2 + gi) * p:(3 + gi) * p]], axis=0)
        y = jnp.dot(t_ref[gi], _group_inputs(x_ref, gi), preferred_element_type=F32)
        y += jnp.dot(wy_ref[gi], hg, preferred_element_type=F32)
        y_ref[0, :, gi * S5_GROUP:(gi + 1) * S5_GROUP, :] = y.reshape(q, S5_GROUP, nc)


def _s5_out_kernel(y_ref, x_ref, w_ref, b_ref, gate_ref, o_ref, *tiles):
    d = w_ref.shape[0]
    nt, nc = y_ref.shape[1], y_ref.shape[3]
    tg = S5_OUT_OFFSETS
    _stage_lane_tiles(x_ref, tiles)
    for j0 in range(0, nt, tg):
        g = jnp.concatenate([jax.nn.gelu(y_ref[0, j0 + j]).T for j in range(tg)], axis=0)
        ab = jnp.dot(g.astype(BF16), w_ref[...], preferred_element_type=F32) + b_ref[...]
        mix = ab[:, :d] * _sigmoid(ab[:, d:])
        for j in range(tg):
            out = (_strided_rows(tiles, j0 + j, nc, nt)
                   + gate_ref[0] * mix[j * nc:(j + 1) * nc])
            _set_strided_rows(tiles, j0 + j, nc, nt, out)
    _unstage_lane_tiles(tiles, o_ref)


def _s5_tables(lam_re, lam_im, log_dt, b_re, b_im, c_re, c_im, d_skip):
    g, p = lam_re.shape
    q, hh = S5_Q, S5_GROUP
    dt = jnp.exp(log_dt)[:, None]
    ld_re, ld_im = lam_re * dt, lam_im * dt

    def lam_pow(k):
        mag = jnp.exp(ld_re[..., None] * k)
        ang = ld_im[..., None] * k
        return mag * jnp.cos(ang), mag * jnp.sin(ang)

    pw_re, pw_im = lam_pow(jnp.arange(q + 1, dtype=F32))
    lb_re, lb_im = pw_re[..., 1], pw_im[..., 1]
    den = lam_re * lam_re + lam_im * lam_im
    f_re = ((lb_re - 1.0) * lam_re + lb_im * lam_im) / den
    f_im = (lb_im * lam_re - (lb_re - 1.0) * lam_im) / den
    bb_re = f_re[..., None] * b_re - f_im[..., None] * b_im
    bb_im = f_re[..., None] * b_im + f_im[..., None] * b_re

    kp_re = jnp.transpose(pw_re[..., :q], (0, 2, 1))[:, :, None, :]
    kp_im = jnp.transpose(pw_im[..., :q], (0, 2, 1))[:, :, None, :]
    cp_re = c_re[:, None] * kp_re - c_im[:, None] * kp_im
    cp_im = c_re[:, None] * kp_im + c_im[:, None] * kp_re
    lag = (jnp.einsum('gkop,gpi->gkoi', cp_re, bb_re, precision=HIGHEST)
           - jnp.einsum('gkop,gpi->gkoi', cp_im, bb_im, precision=HIGHEST))
    lag = lag.at[:, 0].add(d_skip.reshape(g, hh)[:, :, None] * jnp.eye(hh, dtype=F32))
    tt = jnp.arange(q)
    sub_diag = (tt[None, :, None] - tt[None, None, :] == tt[:, None, None]).astype(F32)
    toep = jnp.einsum('kut,gkoh->guoth', sub_diag, lag, precision=HIGHEST).reshape(g, q * hh, q * hh)

    rv_re = jnp.repeat(pw_re[..., :q][..., ::-1], hh, axis=-1)
    rv_im = jnp.repeat(pw_im[..., :q][..., ::-1], hh, axis=-1)
    bt_re, bt_im = jnp.tile(bb_re, (1, 1, q)), jnp.tile(bb_im, (1, 1, q))
    ws = jnp.concatenate([rv_re * bt_re - rv_im * bt_im, rv_re * bt_im + rv_im * bt_re], axis=1)

    up_re = jnp.transpose(pw_re[..., 1:], (0, 2, 1))[:, :, None, :]
    up_im = jnp.transpose(pw_im[..., 1:], (0, 2, 1))[:, :, None, :]
    cy_re = (c_re[:, None] * up_re - c_im[:, None] * up_im).reshape(g, q * hh, p)
    cy_im = (c_re[:, None] * up_im + c_im[:, None] * up_re).reshape(g, q * hh, p)
    wy = jnp.concatenate([cy_re, -cy_im], axis=2)

    a_p = jnp.concatenate([pw_re[..., q].reshape(g // 2, 2 * p),
                           pw_im[..., q].reshape(g // 2, 2 * p)], axis=1)
    return toep.astype(BF16), ws.astype(BF16), wy.astype(BF16), a_p


def _s5_layer(x, norm_g, shift, scale, gate, w_in, lam_re, lam_im, log_dt,
              b_re, b_im, c_re, c_im, d_skip, w_glu, b_glu):
    bsz, seq, d = x.shape
    groups = d // S5_GROUP
    pairs = groups // 2
    nc = seq // S5_Q
    pair_rows = 2 * S5_GROUP
    per_b = pl.BlockSpec((1, 1, d), lambda b, i, k: (b, 0, 0))

    xv = x.reshape(bsz, nc, S5_Q, d)
    tg = V7X_SUBLANES

    def tok_spec(ncb):
        return pl.BlockSpec((1, ncb, tg, d), lambda b, i, k: (b, i, k, 0))

    def chan_spec(ncb):
        return pl.BlockSpec((1, tg, d, ncb), lambda b, i, k: (b, k, 0, i))

    def lane_tiles(ncb):
        return [pltpu.VMEM((ncb * tg, V7X_LANES), F32)] * (d // V7X_LANES)

    ncb = S5_IN_CHUNKS
    ut = pl.pallas_call(
        _s5_in_kernel,
        grid=(bsz, nc // ncb, S5_Q // tg),
        in_specs=[tok_spec(ncb), _const_spec((1, d)), per_b, per_b, _const_spec((d, d))],
        out_specs=chan_spec(ncb),
        out_shape=jax.ShapeDtypeStruct((bsz, S5_Q, d, nc), BF16),
        scratch_shapes=lane_tiles(ncb),
        compiler_params=_compiler_params(("parallel", "parallel", "parallel"),
                                         _nbytes((ncb, tg, d), F32) + _nbytes((tg, d, ncb), BF16),
                                         _nbytes((d, d), BF16) + _nbytes((ncb, tg, d), F32)
                                         + 8 * _nbytes((ncb, d), F32)),
        name="s5_in",
    )(xv, norm_g.reshape(1, d), shift, scale, w_in.astype(BF16))

    toep, ws, wy, a_pair = _s5_tables(lam_re, lam_im, log_dt, b_re, b_im, c_re, c_im, d_skip)

    rows = bsz * pairs
    xspec = pl.BlockSpec((1, S5_Q, pair_rows, nc), lambda b, p: (b, 0, p, 0))
    sspec = pl.BlockSpec((nc, S5_PAIR_STATE), lambda b, p: (0, b * pairs + p))

    def table_spec(r, c):
        return pl.BlockSpec((2, r, c), lambda b, p: (p, 0, 0))

    s_loc = pl.pallas_call(
        _s5_state_kernel,
        grid=(bsz, pairs),
        in_specs=[xspec, table_spec(2 * S5_STATE, S5_QW)],
        out_specs=sspec,
        out_shape=jax.ShapeDtypeStruct((nc, rows * S5_PAIR_STATE), F32),
        compiler_params=_compiler_params(
            ("parallel", "parallel"),
            _nbytes((nc, 2 * S5_QW), BF16) + _nbytes((nc, S5_PAIR_STATE), F32),
            3 * _nbytes((nc, S5_PAIR_STATE), F32)),
        name="s5_chunk_state",
    )(ut, ws)

    cb = SCAN_BLOCK
    sblk = pl.BlockSpec((cb, rows, S5_PAIR_STATE), lambda i: (i, 0, 0))
    h_in = pl.pallas_call(
        _s5_scan_kernel,
        grid=(nc // cb,),
        in_specs=[_const_spec((rows, S5_PAIR_STATE)), sblk],
        out_specs=sblk,
        out_shape=jax.ShapeDtypeStruct((nc, rows, S5_PAIR_STATE), F32),
        scratch_shapes=[pltpu.VMEM((rows, S5_PAIR_STATE), F32)],
        compiler_params=_compiler_params(("arbitrary",),
                                         2 * _nbytes((cb, rows, S5_PAIR_STATE), F32),
                                         1024 * 1024),
        name="s5_chunk_scan",
    )(jnp.tile(a_pair, (bsz, 1)), s_loc.reshape(nc, rows, S5_PAIR_STATE))

    yt = pl.pallas_call(
        _s5_y_kernel,
        grid=(bsz, pairs),
        in_specs=[xspec, sspec, table_spec(S5_QW, S5_QW), table_spec(S5_QW, 2 * S5_STATE)],
        out_specs=xspec,
        out_shape=jax.ShapeDtypeStruct((bsz, S5_Q, d, nc), F32),
        compiler_params=_compiler_params(
            ("parallel", "parallel"),
            _nbytes((nc, 2 * S5_QW), BF16) + _nbytes((nc, 2 * S5_QW), F32)
            + _nbytes((nc, S5_PAIR_STATE), F32),
            3 * _nbytes((nc, 2 * S5_QW), F32)),
        name="s5_chunk_output",
    )(ut, h_in.reshape(nc, rows * S5_PAIR_STATE), toep, wy)

    ncb = S5_OUT_CHUNKS
    out = pl.pallas_call(
        _s5_out_kernel,
        grid=(bsz, nc // ncb, S5_Q // tg),
        in_specs=[chan_spec(ncb), tok_spec(ncb), _const_spec((d, 2 * d)),
                  _const_spec((1, 2 * d)), per_b],
        out_specs=tok_spec(ncb),
        out_shape=jax.ShapeDtypeStruct(xv.shape, F32),
        scratch_shapes=lane_tiles(ncb),
        compiler_params=_compiler_params(
            ("parallel", "parallel", "parallel"),
            3 * _nbytes((ncb, tg, d), F32),
            _nbytes((d, 2 * d), BF16) + _nbytes((ncb, tg, d), F32)
            + 6 * _nbytes((S5_OUT_OFFSETS * ncb, 2 * d), F32)),
        name="s5_out",
    )(yt, xv, w_glu.astype(BF16), b_glu.reshape(1, 2 * d), gate)
    return out.reshape(bsz, seq, d)


def _ssd_kernel(x_ref, g_ref, sh_ref, sc_ref, gate_ref, win_ref, wdtt_ref, cw_ref, cb_ref,
                biast_ref, alogt_ref, dskip_ref, ng_ref, wo_ref, o_ref,
                h_ref, xr_ref, dtt_ref, y_ref, halo_ref, prev_ref, st_ref, *tiles, d_inner, heads):
    q = M2_CHUNK
    n = M2_STATE
    hp = M2_HEAD_PAIR
    ph_n = M2_PHASES
    rpc = M2_PHASE_ROWS
    kc = M2_BLOCK_CHUNKS
    halo = V7X_SUBLANES
    n_prev = M2_CONV - 1

    @pl.when(pl.program_id(1) == 0)
    def _():
        halo_ref[:, halo - 1:halo, :] = jnp.zeros((n_prev, 1, halo_ref.shape[2]), F32)
        st_ref[...] = jnp.zeros_like(st_ref)

    _stage_lane_tiles(x_ref, tiles)
    xb = jnp.concatenate([_strided_rows(tiles, k * q + ph, rpc, ph_n)
                          for k in range(kc) for ph in range(ph_n)], axis=0)
    h = _rms_mod(xb, g_ref[...], sh_ref[0], sc_ref[0]).astype(BF16)
    h_ref[...] = h
    conv_dim = xr_ref.shape[2]
    xr = jnp.dot(h, win_ref[:, d_inner:d_inner + conv_dim], preferred_element_type=F32)
    dtt = _softplus(lax.dot_general(wdtt_ref[...], h, (((1,), (1,)), ((), ())),
                                    preferred_element_type=F32) + biast_ref[...])
    for k in range(kc):
        dtt_ref[k] = dtt[:, k * q:(k + 1) * q]

    for k in range(kc):
        xr_ref[k] = xr[k * q:(k + 1) * q, :]
    for j in range(n_prev):
        ph = ph_n - n_prev + j
        for k in range(kc):
            halo_ref[j, halo + k * rpc:halo + (k + 1) * rpc, :] = (
                xr[k * q + ph * rpc:k * q + (ph + 1) * rpc, :])
    for j in range(n_prev):
        back = halo_ref[j, pl.ds(halo - 1, kc * rpc), :]
        for k in range(kc):
            prev_ref[k, j * rpc:(j + 1) * rpc, :] = back[k * rpc:(k + 1) * rpc]
        halo_ref[j, halo - 1:halo, :] = halo_ref[j, halo + kc * rpc - 1:halo + kc * rpc, :]

    shift_bits = rpc.bit_length() - 1
    row = lax.broadcasted_iota(jnp.int32, (q, q), 0)
    col = lax.broadcasted_iota(jnp.int32, (q, q), 1)
    pos_r = (row & (rpc - 1)) * ph_n + lax.shift_right_logical(row, shift_bits)
    pos_c = (col & (rpc - 1)) * ph_n + lax.shift_right_logical(col, shift_bits)
    causal = pos_r >= pos_c
    cum_mat = (pos_r <= pos_c).astype(F32)
    neg_a = -jnp.exp(alogt_ref[...])
    lane = lax.broadcasted_iota(jnp.int32, (q, hp), 1)
    first = lane < M2_HEADDIM
    first_row = lax.broadcasted_iota(jnp.int32, (1, hp), 1) < M2_HEADDIM
    heads_per_group = heads // M2_GROUPS
    pairs_per_group = heads_per_group // 2

    def core(k):
        xk = xr_ref[k]
        pk = prev_ref[k]
        acc = cb_ref[...] + cw_ref[M2_CONV - 1:M2_CONV, :] * xk
        for s in range(1, M2_CONV):
            shifted = jnp.concatenate([pk[(n_prev - s) * rpc:], xk[:q - s * rpc]], axis=0)
            acc += cw_ref[M2_CONV - 1 - s:M2_CONV - s, :] * shifted
        xbc = acc * _sigmoid(acc)
        dtk = dtt_ref[k]
        a_cst = LOG2_E * jnp.dot(dtk * neg_a, cum_mat, precision=HIGHEST,
                                 preferred_element_type=F32)
        src = a_cst - LOG2_E * jnp.log(dtk)
        a_cs = jnp.concatenate([a_cst, jnp.zeros((q - heads, q), F32)], axis=0).T
        state_w = jnp.exp2(a_cst[:, q - 1:q] - a_cst) * dtk
        chunk_decay = jnp.exp2(a_cs[q - 1:q, :])
        for g in range(M2_GROUPS):
            b_g = xbc[:, d_inner + g * n:d_inner + (g + 1) * n]
            c_g = xbc[:, d_inner + M2_GROUPS * n + g * n:d_inner + M2_GROUPS * n + (g + 1) * n]
            cb = lax.dot_general(c_g.astype(BF16), b_g.astype(BF16), (((1,), (1,)), ((), ())),
                                 preferred_element_type=F32)
            b_t = b_g.T
            for j in range(pairs_per_group):
                h0 = g * heads_per_group + 2 * j
                sl = slice((h0 // 2) * hp, (h0 // 2 + 1) * hp)
                lhs, lhs_state = [], []
                for hd in (h0, h0 + 1):
                    bc = jnp.broadcast_to(a_cs[:, hd:hd + 1], (q, q))
                    decay_dt = jnp.exp2(jnp.where(causal, bc - src[hd:hd + 1, :], -jnp.inf))
                    lhs.append(cb * decay_dt)
                    lhs_state.append(b_t * state_w[hd:hd + 1, :])
                for hd in (h0, h0 + 1):
                    lhs.append(c_g * jnp.exp2(jnp.broadcast_to(a_cs[:, hd:hd + 1], (q, n))))
                xs_p = xbc[:, sl]
                st_p = st_ref[:, sl]
                x0 = jnp.where(first, xs_p, 0.0)
                x1 = jnp.where(first, 0.0, xs_p)
                s0 = jnp.where(first, st_p, 0.0)
                s1 = jnp.where(first, 0.0, st_p)
                lhs = jnp.concatenate(lhs, axis=1).astype(BF16)
                rhs = jnp.concatenate([x0, x1, s0, s1], axis=0).astype(BF16)
                y = jnp.dot(lhs, rhs, preferred_element_type=F32)
                y_ref[k, :, sl] = y + xs_p * dskip_ref[:, sl]
                s_new = jnp.dot(jnp.concatenate(lhs_state, axis=1).astype(BF16),
                                jnp.concatenate([x0, x1], axis=0).astype(BF16),
                                preferred_element_type=F32)
                cd = jnp.where(first_row, chunk_decay[:, h0:h0 + 1], chunk_decay[:, h0 + 1:h0 + 2])
                st_ref[:, sl] = st_p * cd + s_new

    def post(k):
        z = jnp.dot(h_ref[pl.ds(k * q, q), :], win_ref[:, :d_inner],
                    preferred_element_type=F32)
        y = y_ref[k] * (z * _sigmoid(z))
        gw = d_inner // M2_GROUPS
        parts = []
        for g in range(M2_GROUPS):
            yg = y[:, g * gw:(g + 1) * gw]
            parts.append(yg * lax.rsqrt(jnp.mean(yg * yg, axis=-1, keepdims=True) + NORM_EPS))
        y = jnp.concatenate(parts, axis=1) * ng_ref[...]
        out = jnp.dot(y.astype(BF16), wo_ref[...], preferred_element_type=F32)
        for ph in range(ph_n):
            res = (_strided_rows(tiles, k * q + ph, rpc, ph_n)
                   + gate_ref[0] * out[ph * rpc:(ph + 1) * rpc])
            _set_strided_rows(tiles, k * q + ph, rpc, ph_n, res)

    core(0)

    for k in range(1, kc):
        post(k - 1)
        core(k)
    post(kc - 1)
    _unstage_lane_tiles(tiles, o_ref)


def _ssd_layer(x, norm_g, shift, scale, gate, w_in, conv_w, conv_b, dt_bias, a_log,
               d_skip, gn_g, w_out):
    bsz, seq, d = x.shape
    d_inner = w_out.shape[0]
    heads = d_inner // M2_HEADDIM
    conv_dim = d_inner + 2 * M2_GROUPS * M2_STATE
    q = M2_CHUNK
    kc = M2_BLOCK_CHUNKS
    rows = kc * M2_PHASE_ROWS

    w_dt_t = w_in[:, d_inner + conv_dim:].T.astype(BF16)

    xv = x.reshape(bsz, seq // M2_PHASES, M2_PHASES, d)
    blk = pl.BlockSpec((1, rows, M2_PHASES, d), lambda b, i: (b, i, 0, 0))
    per_b = pl.BlockSpec((1, 1, d), lambda b, i: (b, 0, 0))
    out = pl.pallas_call(
        functools.partial(_ssd_kernel, d_inner=d_inner, heads=heads),
        grid=(bsz, seq // (kc * q)),
        in_specs=[blk, _const_spec((1, d)), per_b, per_b, per_b,
                  _const_spec(w_in.shape), _const_spec((heads, d)),
                  _const_spec((M2_CONV, conv_dim)), _const_spec((1, conv_dim)),
                  _const_spec((heads, 1)), _const_spec((heads, 1)),
                  _const_spec((1, d_inner)), _const_spec((1, d_inner)), _const_spec((d_inner, d))],
        out_specs=blk,
        out_shape=jax.ShapeDtypeStruct(xv.shape, F32),
        scratch_shapes=[pltpu.VMEM((kc * q, d), BF16),
                        pltpu.VMEM((kc, q, conv_dim), F32),
                        pltpu.VMEM((kc, heads, q), F32),
                        pltpu.VMEM((kc, q, d_inner), F32),
                        pltpu.VMEM((M2_CONV - 1, V7X_SUBLANES + rows, conv_dim), F32),
                        pltpu.VMEM((kc, (M2_CONV - 1) * M2_PHASE_ROWS, conv_dim), F32),
                        pltpu.VMEM((M2_STATE, d_inner), F32)]
        + [pltpu.VMEM((kc * q, V7X_LANES), F32)] * (d // V7X_LANES),
        compiler_params=_compiler_params(
            ("arbitrary", "arbitrary"),
            2 * _nbytes((kc * q, d), F32),
            _nbytes((d, 2 * d_inner + conv_dim), BF16)
            + 2 * _nbytes((kc * q, conv_dim + d_inner), F32)
            + 2 * _nbytes((kc * q, conv_dim), F32)),
        name="ssd_mixer",
    )(xv, norm_g.reshape(1, d), shift, scale, gate, w_in.astype(BF16), w_dt_t,
      conv_w, conv_b.reshape(1, conv_dim), dt_bias.reshape(heads, 1), a_log.reshape(heads, 1),
      jnp.repeat(d_skip, M2_HEADDIM).reshape(1, d_inner), gn_g.reshape(1, d_inner),
      w_out.astype(BF16))
    return out.reshape(bsz, seq, d)


def kernel(x, c, ada_w, ada_b, norm_mix_g, norm_mlp_g, mlp_w1, mlp_w2, s5_w_in, s5_lambda_re, s5_lambda_im, s5_log_dt, s5_b_re, s5_b_im, s5_c_re, s5_c_im, s5_d, s5_w_glu, s5_b_glu, m2_w_in, m2_conv_w, m2_conv_b, m2_dt_bias, m2_a_log, m2_d, m2_norm_g, m2_w_out, final_norm_g):
    depth = ada_w.shape[0]
    bsz, _, d = x.shape
    mod = _ada_modulation(c, ada_w, ada_b).reshape(depth, bsz, N_MOD, 1, d)
    w1_all, w2_all = mlp_w1.astype(BF16), mlp_w2.astype(BF16)
    for i in range(depth):
        sh1, sc1, g1, sh2, sc2, g2 = (mod[i, :, k] for k in range(N_MOD))
        j = i // 2
        if i % 2 == 0:
            x = _s5_layer(x, norm_mix_g[i], sh1, sc1, g1, s5_w_in[j], s5_lambda_re[j],
                          s5_lambda_im[j], s5_log_dt[j], s5_b_re[j], s5_b_im[j],
                          s5_c_re[j], s5_c_im[j], s5_d[j], s5_w_glu[j], s5_b_glu[j])
        else:
            x = _ssd_layer(x, norm_mix_g[i], sh1, sc1, g1, m2_w_in[j], m2_conv_w[j],
                           m2_conv_b[j], m2_dt_bias[j], m2_a_log[j], m2_d[j],
                           m2_norm_g[j], m2_w_out[j])
        x = _mlp_layer(x, norm_mlp_g[i], sh2, sc2, g2, w1_all, w2_all, i,
                       final_g=final_norm_g if i == depth - 1 else None)
    return x
```

```python
import functools

import jax
import jax.numpy as jnp
from jax import lax
from jax.experimental import pallas as pl
from jax.experimental.pallas import tpu as pltpu

F32 = jnp.float32
BF16 = jnp.bfloat16
HIGHEST = lax.Precision.HIGHEST
NORM_EPS = 1e-5
N_MOD = 6
LOG2_E = 1.4426950408889634

V7X_LANES = 128
V7X_SUBLANES = 8
V7X_SCOPED_VMEM_BYTES = 60000 * 1024

S5_GROUP = 16
S5_STATE = 64
S5_Q = 16
S5_QW = S5_Q * S5_GROUP
S5_PAIR_STATE = 4 * S5_STATE
S5_IN_CHUNKS = 256
S5_OUT_CHUNKS = 128
S5_OUT_OFFSETS = 4
S5_STEP_PAIRS = 2

M2_HEADDIM = 64
M2_GROUPS = 4
M2_STATE = 128
M2_CONV = 4
M2_CHUNK = 128
M2_HEAD_PAIR = 2 * M2_HEADDIM
M2_PHASES = V7X_SUBLANES
M2_PHASE_ROWS = M2_CHUNK // M2_PHASES
M2_BLOCK_CHUNKS = 4

TOKEN_BLOCK = 512
SCAN_BLOCK = 64


def _compiler_params(semantics, block_bytes, temp_bytes):
    want = 2 * block_bytes + temp_bytes
    return pltpu.CompilerParams(
        dimension_semantics=semantics,
        vmem_limit_bytes=int(min(V7X_SCOPED_VMEM_BYTES, max(want, 16 * 1024 * 1024))))


def _nbytes(shape, dtype):
    n = 1
    for s in shape:
        n *= s
    return n * jnp.dtype(dtype).itemsize


def _const_spec(shape):
    return pl.BlockSpec(shape, lambda *_: (0,) * len(shape), pipeline_mode=pl.Buffered(1))


def _rms_mod(x, g, shift, scale):
    y = x * lax.rsqrt(jnp.mean(x * x, axis=-1, keepdims=True) + NORM_EPS)
    return (y * g) * (1.0 + scale) + shift


def _softplus(x):
    return jnp.maximum(x, 0.0) + jnp.log1p(jnp.exp(-jnp.abs(x)))


def _sigmoid(x):
    return 0.5 * jnp.tanh(0.5 * x) + 0.5


def _ada_kernel(c_ref, w_ref, b_ref, o_ref):
    cond = jax.nn.silu(c_ref[...])
    o_ref[0] = jnp.dot(cond, w_ref[0], precision=HIGHEST,
                       preferred_element_type=F32) + b_ref[0]


def _ada_modulation(c, ada_w, ada_b):
    depth, d, n = ada_w.shape
    bsz = c.shape[0]
    rows = V7X_SUBLANES
    nb = n // 4
    c_pad = jnp.zeros((rows, d), F32).at[:bsz].set(c)
    out = pl.pallas_call(
        _ada_kernel,
        grid=(depth, n // nb),
        in_specs=[pl.BlockSpec((rows, d), lambda i, j: (0, 0)),
                  pl.BlockSpec((1, d, nb), lambda i, j: (i, 0, j)),
                  pl.BlockSpec((1, 1, nb), lambda i, j: (i, 0, j))],
        out_specs=pl.BlockSpec((1, rows, nb), lambda i, j: (i, 0, j)),
        out_shape=jax.ShapeDtypeStruct((depth, rows, n), F32),
        compiler_params=_compiler_params(("arbitrary", "arbitrary"),
                                         _nbytes((d, nb), F32), 4 * 1024 * 1024),
        name="ada_modulation",
    )(c_pad, ada_w, ada_b.reshape(depth, 1, n))
    return out[:, :bsz, :]


def _mlp_kernel(x_ref, g_ref, sh_ref, sc_ref, gate_ref, w1_ref, w2_ref, *rest, final):
    if final:
        fg_ref, o_ref = rest
    else:
        (o_ref,) = rest
    x = x_ref[0]
    h = _rms_mod(x, g_ref[...], sh_ref[0], sc_ref[0])
    a = jnp.maximum(jnp.dot(h.astype(BF16), w1_ref[0], preferred_element_type=F32), 0.0)
    y = jnp.dot((a * a).astype(BF16), w2_ref[0], preferred_element_type=F32)
    out = x + gate_ref[0] * y
    if final:
        out = out * lax.rsqrt(jnp.mean(out * out, axis=-1, keepdims=True) + NORM_EPS)
        out = out * fg_ref[...]
    o_ref[0] = out


def _mlp_layer(x, norm_g, shift, scale, gate, w1_all, w2_all, layer, final_g=None):
    bsz, seq, d = x.shape
    f = w1_all.shape[2]
    tl = TOKEN_BLOCK
    tok = pl.BlockSpec((1, tl, d), lambda b, i: (b, i, 0))
    per_b = pl.BlockSpec((1, 1, d), lambda b, i: (b, 0, 0))

    def layer_spec(r, c):
        return pl.BlockSpec((1, r, c), lambda b, i: (layer, 0, 0), pipeline_mode=pl.Buffered(1))

    in_specs = [tok, _const_spec((1, d)), per_b, per_b, per_b, layer_spec(d, f), layer_spec(f, d)]
    args = [x, norm_g.reshape(1, d), shift, scale, gate, w1_all, w2_all]
    if final_g is not None:
        in_specs.append(_const_spec((1, d)))
        args.append(final_g.reshape(1, d))
    return pl.pallas_call(
        functools.partial(_mlp_kernel, final=final_g is not None),
        grid=(bsz, seq // tl),
        in_specs=in_specs,
        out_specs=tok,
        out_shape=jax.ShapeDtypeStruct(x.shape, F32),
        compiler_params=_compiler_params(
            ("parallel", "parallel"),
            2 * _nbytes((tl, d), F32) + _nbytes((d, f), BF16),
            _nbytes((tl, f), F32) * 2 + _nbytes((tl, d), F32) * 2),
        name="mlp",
    )(*args)


def _stage_lane_tiles(x_ref, tiles):
    r, s, _ = x_ref.shape[1:]
    for j, t_ref in enumerate(tiles):
        t_ref[...] = x_ref[0, :, :, j * V7X_LANES:(j + 1) * V7X_LANES].reshape(r * s, V7X_LANES)


def _unstage_lane_tiles(tiles, o_ref):
    r, s, _ = o_ref.shape[1:]
    for j, t_ref in enumerate(tiles):
        o_ref[0, :, :, j * V7X_LANES:(j + 1) * V7X_LANES] = t_ref[...].reshape(r, s, V7X_LANES)


def _strided_rows(tiles, start, size, stride):
    return jnp.concatenate([t[pl.ds(start, size, stride=stride), :] for t in tiles], axis=1)


def _set_strided_rows(tiles, start, size, stride, value):
    for j, t_ref in enumerate(tiles):
        t_ref[pl.ds(start, size, stride=stride), :] = value[:, j * V7X_LANES:(j + 1) * V7X_LANES]


def _s5_in_kernel(x_ref, g_ref, sh_ref, sc_ref, w_ref, u_ref, *tiles):
    nc, tg = x_ref.shape[1:3]
    _stage_lane_tiles(x_ref, tiles)
    for j in range(tg):
        h = _rms_mod(_strided_rows(tiles, j, nc, tg), g_ref[...], sh_ref[0], sc_ref[0])
        u = jnp.dot(h.astype(BF16), w_ref[...], preferred_element_type=F32)
        u_ref[0, j] = u.T.astype(BF16)


def _group_inputs(x_ref, gi):
    q, _, nc = x_ref.shape[1:]
    return x_ref[0, :, gi * S5_GROUP:(gi + 1) * S5_GROUP, :].reshape(q * S5_GROUP, nc)


def _s5_state_kernel(x_ref, ws_ref, s_ref):
    p = S5_STATE
    for pp in range(ws_ref.shape[0] // 2):
        parts = [jnp.dot(ws_ref[2 * pp + gi], _group_inputs(x_ref, 2 * pp + gi),
                         preferred_element_type=F32)
                 for gi in range(2)]
        st = jnp.concatenate([parts[0][:p], parts[1][:p], parts[0][p:], parts[1][p:]], axis=0)
        s_ref[:, pp * S5_PAIR_STATE:(pp + 1) * S5_PAIR_STATE] = st.T


def _s5_scan_kernel(a_ref, s_ref, o_ref, h_ref):
    half = a_ref.shape[1] // 2

    @pl.when(pl.program_id(0) == 0)
    def _():
        h_ref[...] = jnp.zeros_like(h_ref)

    a_re = a_ref[:, :half]
    a_im = a_ref[:, half:]

    def body(c, carry):
        h_re, h_im = carry
        o_ref[c, :, :half] = h_re
        o_ref[c, :, half:] = h_im
        s = s_ref[c]
        return (a_re * h_re - a_im * h_im + s[:, :half],
                a_re * h_im + a_im * h_re + s[:, half:])

    h_re, h_im = lax.fori_loop(0, s_ref.shape[0], body,
                               (h_ref[:, :half], h_ref[:, half:]))
    h_ref[:, :half] = h_re
    h_ref[:, half:] = h_im


def _s5_y_kernel(x_ref, h_ref, t_ref, wy_ref, y_ref):
    q = x_ref.shape[1]
    nc = x_ref.shape[3]
    p = S5_STATE
    for pp in range(t_ref.shape[0] // 2):
        ht = h_ref[:, pp * S5_PAIR_STATE:(pp + 1) * S5_PAIR_STATE].T.astype(BF16)
        for gi in range(2):
            g = 2 * pp + gi
            hg = jnp.concatenate([ht[gi * p:(gi + 1) * p], ht[(2 + gi) * p:(3 + gi) * p]], axis=0)
            y = jnp.dot(t_ref[g], _group_inputs(x_ref, g), preferred_element_type=F32)
            y += jnp.dot(wy_ref[g], hg, preferred_element_type=F32)
            y_ref[0, :, g * S5_GROUP:(g + 1) * S5_GROUP, :] = y.reshape(q, S5_GROUP, nc)


def _s5_out_kernel(y_ref, x_ref, w_ref, b_ref, gate_ref, o_ref, *tiles):
    d = w_ref.shape[0]
    nt, nc = y_ref.shape[1], y_ref.shape[3]
    tg = S5_OUT_OFFSETS
    _stage_lane_tiles(x_ref, tiles)
    for j0 in range(0, nt, tg):
        g = jnp.concatenate([jax.nn.gelu(y_ref[0, j0 + j]).T for j in range(tg)], axis=0)
        ab = jnp.dot(g.astype(BF16), w_ref[...], preferred_element_type=F32) + b_ref[...]
        mix = ab[:, :d] * _sigmoid(ab[:, d:])
        for j in range(tg):
            out = (_strided_rows(tiles, j0 + j, nc, nt)
                   + gate_ref[0] * mix[j * nc:(j + 1) * nc])
            _set_strided_rows(tiles, j0 + j, nc, nt, out)
    _unstage_lane_tiles(tiles, o_ref)


def _s5_tables(lam_re, lam_im, log_dt, b_re, b_im, c_re, c_im, d_skip):
    g, p = lam_re.shape
    q, hh = S5_Q, S5_GROUP
    dt = jnp.exp(log_dt)[:, None]
    ld_re, ld_im = lam_re * dt, lam_im * dt

    def lam_pow(k):
        mag = jnp.exp(ld_re[..., None] * k)
        ang = ld_im[..., None] * k
        return mag * jnp.cos(ang), mag * jnp.sin(ang)

    pw_re, pw_im = lam_pow(jnp.arange(q + 1, dtype=F32))
    lb_re, lb_im = pw_re[..., 1], pw_im[..., 1]
    den = lam_re * lam_re + lam_im * lam_im
    f_re = ((lb_re - 1.0) * lam_re + lb_im * lam_im) / den
    f_im = (lb_im * lam_re - (lb_re - 1.0) * lam_im) / den
    bb_re = f_re[..., None] * b_re - f_im[..., None] * b_im
    bb_im = f_re[..., None] * b_im + f_im[..., None] * b_re

    kp_re = jnp.transpose(pw_re[..., :q], (0, 2, 1))[:, :, None, :]
    kp_im = jnp.transpose(pw_im[..., :q], (0, 2, 1))[:, :, None, :]
    cp_re = c_re[:, None] * kp_re - c_im[:, None] * kp_im
    cp_im = c_re[:, None] * kp_im + c_im[:, None] * kp_re
    lag = (jnp.einsum('gkop,gpi->gkoi', cp_re, bb_re, precision=HIGHEST)
           - jnp.einsum('gkop,gpi->gkoi', cp_im, bb_im, precision=HIGHEST))
    lag = lag.at[:, 0].add(d_skip.reshape(g, hh)[:, :, None] * jnp.eye(hh, dtype=F32))
    tt = jnp.arange(q)
    sub_diag = (tt[None, :, None] - tt[None, None, :] == tt[:, None, None]).astype(F32)
    toep = jnp.einsum('kut,gkoh->guoth', sub_diag, lag, precision=HIGHEST).reshape(g, q * hh, q * hh)

    rv_re = jnp.repeat(pw_re[..., :q][..., ::-1], hh, axis=-1)
    rv_im = jnp.repeat(pw_im[..., :q][..., ::-1], hh, axis=-1)
    bt_re, bt_im = jnp.tile(bb_re, (1, 1, q)), jnp.tile(bb_im, (1, 1, q))
    ws = jnp.concatenate([rv_re * bt_re - rv_im * bt_im, rv_re * bt_im + rv_im * bt_re], axis=1)

    up_re = jnp.transpose(pw_re[..., 1:], (0, 2, 1))[:, :, None, :]
    up_im = jnp.transpose(pw_im[..., 1:], (0, 2, 1))[:, :, None, :]
    cy_re = (c_re[:, None] * up_re - c_im[:, None] * up_im).reshape(g, q * hh, p)
    cy_im = (c_re[:, None] * up_im + c_im[:, None] * up_re).reshape(g, q * hh, p)
    wy = jnp.concatenate([cy_re, -cy_im], axis=2)

    a_p = jnp.concatenate([pw_re[..., q].reshape(g // 2, 2 * p),
                           pw_im[..., q].reshape(g // 2, 2 * p)], axis=1)
    return toep.astype(BF16), ws.astype(BF16), wy.astype(BF16), a_p


def _s5_layer(x, norm_g, shift, scale, gate, w_in, lam_re, lam_im, log_dt,
              b_re, b_im, c_re, c_im, d_skip, w_glu, b_glu):
    bsz, seq, d = x.shape
    groups = d // S5_GROUP
    pairs = groups // 2
    nc = seq // S5_Q
    pair_rows = 2 * S5_GROUP
    per_b = pl.BlockSpec((1, 1, d), lambda b, i, k: (b, 0, 0))

    xv = x.reshape(bsz, nc, S5_Q, d)
    tg = V7X_SUBLANES

    def tok_spec(ncb):
        return pl.BlockSpec((1, ncb, tg, d), lambda b, i, k: (b, i, k, 0))

    def chan_spec(ncb):
        return pl.BlockSpec((1, tg, d, ncb), lambda b, i, k: (b, k, 0, i))

    def lane_tiles(ncb):
        return [pltpu.VMEM((ncb * tg, V7X_LANES), F32)] * (d // V7X_LANES)

    ncb = S5_IN_CHUNKS
    ut = pl.pallas_call(
        _s5_in_kernel,
        grid=(bsz, nc // ncb, S5_Q // tg),
        in_specs=[tok_spec(ncb), _const_spec((1, d)), per_b, per_b, _const_spec((d, d))],
        out_specs=chan_spec(ncb),
        out_shape=jax.ShapeDtypeStruct((bsz, S5_Q, d, nc), BF16),
        scratch_shapes=lane_tiles(ncb),
        compiler_params=_compiler_params(("parallel", "parallel", "parallel"),
                                         _nbytes((ncb, tg, d), F32) + _nbytes((tg, d, ncb), BF16),
                                         _nbytes((d, d), BF16) + _nbytes((ncb, tg, d), F32)
                                         + 8 * _nbytes((ncb, d), F32)),
        name="s5_in",
    )(xv, norm_g.reshape(1, d), shift, scale, w_in.astype(BF16))

    toep, ws, wy, a_pair = _s5_tables(lam_re, lam_im, log_dt, b_re, b_im, c_re, c_im, d_skip)

    rows = bsz * pairs
    sp = S5_STEP_PAIRS
    steps = pairs // sp
    xspec = pl.BlockSpec((1, S5_Q, sp * pair_rows, nc), lambda b, p: (b, 0, p, 0))
    sspec = pl.BlockSpec((nc, sp * S5_PAIR_STATE), lambda b, p: (0, b * steps + p))

    def table_spec(r, c):
        return pl.BlockSpec((2 * sp, r, c), lambda b, p: (p, 0, 0))

    s_loc = pl.pallas_call(
        _s5_state_kernel,
        grid=(bsz, steps),
        in_specs=[xspec, table_spec(2 * S5_STATE, S5_QW)],
        out_specs=sspec,
        out_shape=jax.ShapeDtypeStruct((nc, rows * S5_PAIR_STATE), F32),
        compiler_params=_compiler_params(
            ("parallel", "parallel"),
            sp * (_nbytes((nc, 2 * S5_QW), BF16) + _nbytes((nc, S5_PAIR_STATE), F32)),
            3 * _nbytes((nc, S5_PAIR_STATE), F32)),
        name="s5_chunk_state",
    )(ut, ws)

    cb = SCAN_BLOCK
    sblk = pl.BlockSpec((cb, rows, S5_PAIR_STATE), lambda i: (i, 0, 0))
    h_in = pl.pallas_call(
        _s5_scan_kernel,
        grid=(nc // cb,),
        in_specs=[_const_spec((rows, S5_PAIR_STATE)), sblk],
        out_specs=sblk,
        out_shape=jax.ShapeDtypeStruct((nc, rows, S5_PAIR_STATE), F32),
        scratch_shapes=[pltpu.VMEM((rows, S5_PAIR_STATE), F32)],
        compiler_params=_compiler_params(("arbitrary",),
                                         2 * _nbytes((cb, rows, S5_PAIR_STATE), F32),
                                         1024 * 1024),
        name="s5_chunk_scan",
    )(jnp.tile(a_pair, (bsz, 1)), s_loc.reshape(nc, rows, S5_PAIR_STATE))

    yt = pl.pallas_call(
        _s5_y_kernel,
        grid=(bsz, steps),
        in_specs=[xspec, sspec, table_spec(S5_QW, S5_QW), table_spec(S5_QW, 2 * S5_STATE)],
        out_specs=xspec,
        out_shape=jax.ShapeDtypeStruct((bsz, S5_Q, d, nc), F32),
        compiler_params=_compiler_params(
            ("parallel", "parallel"),
            sp * (_nbytes((nc, 2 * S5_QW), BF16) + _nbytes((nc, 2 * S5_QW), F32)
                  + _nbytes((nc, S5_PAIR_STATE), F32)),
            3 * _nbytes((nc, 2 * S5_QW), F32)),
        name="s5_chunk_output",
    )(ut, h_in.reshape(nc, rows * S5_PAIR_STATE), toep, wy)

    ncb = S5_OUT_CHUNKS
    out = pl.pallas_call(
        _s5_out_kernel,
        grid=(bsz, nc // ncb, S5_Q // tg),
        in_specs=[chan_spec(ncb), tok_spec(ncb), _const_spec((d, 2 * d)),
                  _const_spec((1, 2 * d)), per_b],
        out_specs=tok_spec(ncb),
        out_shape=jax.ShapeDtypeStruct(xv.shape, F32),
        scratch_shapes=lane_tiles(ncb),
        compiler_params=_compiler_params(
            ("parallel", "parallel", "parallel"),
            3 * _nbytes((ncb, tg, d), F32),
            _nbytes((d, 2 * d), BF16) + _nbytes((ncb, tg, d), F32)
            + 6 * _nbytes((S5_OUT_OFFSETS * ncb, 2 * d), F32)),
        name="s5_out",
    )(yt, xv, w_glu.astype(BF16), b_glu.reshape(1, 2 * d), gate)
    return out.reshape(bsz, seq, d)


def _ssd_kernel(x_ref, g_ref, sh_ref, sc_ref, gate_ref, win_ref, wdtt_ref, cw_ref, cb_ref,
                biast_ref, alogt_ref, dskip_ref, ng_ref, wo_ref, o_ref,
                h_ref, xr_ref, dtt_ref, y_ref, halo_ref, prev_ref, st_ref, *tiles, d_inner, heads):
    q = M2_CHUNK
    n = M2_STATE
    hp = M2_HEAD_PAIR
    ph_n = M2_PHASES
    rpc = M2_PHASE_ROWS
    kc = M2_BLOCK_CHUNKS
    halo = V7X_SUBLANES
    n_prev = M2_CONV - 1

    @pl.when(pl.program_id(1) == 0)
    def _():
        halo_ref[:, halo - 1:halo, :] = jnp.zeros((n_prev, 1, halo_ref.shape[2]), F32)
        st_ref[...] = jnp.zeros_like(st_ref)

    _stage_lane_tiles(x_ref, tiles)
    xb = jnp.concatenate([_strided_rows(tiles, k * q + ph, rpc, ph_n)
                          for k in range(kc) for ph in range(ph_n)], axis=0)
    h = _rms_mod(xb, g_ref[...], sh_ref[0], sc_ref[0]).astype(BF16)
    h_ref[...] = h
    conv_dim = xr_ref.shape[2]
    xr = jnp.dot(h, win_ref[:, d_inner:d_inner + conv_dim], preferred_element_type=F32)
    dtt = _softplus(lax.dot_general(wdtt_ref[...], h, (((1,), (1,)), ((), ())),
                                    preferred_element_type=F32) + biast_ref[...])
    for k in range(kc):
        dtt_ref[k] = dtt[:, k * q:(k + 1) * q]

    for k in range(kc):
        xr_ref[k] = xr[k * q:(k + 1) * q, :]
    for j in range(n_prev):
        ph = ph_n - n_prev + j
        for k in range(kc):
            halo_ref[j, halo + k * rpc:halo + (k + 1) * rpc, :] = (
                xr[k * q + ph * rpc:k * q + (ph + 1) * rpc, :])
    for j in range(n_prev):
        back = halo_ref[j, pl.ds(halo - 1, kc * rpc), :]
        for k in range(kc):
            prev_ref[k, j * rpc:(j + 1) * rpc, :] = back[k * rpc:(k + 1) * rpc]
        halo_ref[j, halo - 1:halo, :] = halo_ref[j, halo + kc * rpc - 1:halo + kc * rpc, :]

    shift_bits = rpc.bit_length() - 1
    row = lax.broadcasted_iota(jnp.int32, (q, q), 0)
    col = lax.broadcasted_iota(jnp.int32, (q, q), 1)
    pos_r = (row & (rpc - 1)) * ph_n + lax.shift_right_logical(row, shift_bits)
    pos_c = (col & (rpc - 1)) * ph_n + lax.shift_right_logical(col, shift_bits)
    causal = pos_r >= pos_c
    cum_mat = (pos_r <= pos_c).astype(F32)
    neg_a = -jnp.exp(alogt_ref[...])
    lane = lax.broadcasted_iota(jnp.int32, (q, hp), 1)
    first = lane < M2_HEADDIM
    first_row = lax.broadcasted_iota(jnp.int32, (1, hp), 1) < M2_HEADDIM
    heads_per_group = heads // M2_GROUPS
    pairs_per_group = heads_per_group // 2

    def core(k):
        xk = xr_ref[k]
        pk = prev_ref[k]
        acc = cb_ref[...] + cw_ref[M2_CONV - 1:M2_CONV, :] * xk
        for s in range(1, M2_CONV):
            shifted = jnp.concatenate([pk[(n_prev - s) * rpc:], xk[:q - s * rpc]], axis=0)
            acc += cw_ref[M2_CONV - 1 - s:M2_CONV - s, :] * shifted
        xbc = acc * _sigmoid(acc)
        dtk = dtt_ref[k]
        a_cst = LOG2_E * jnp.dot(dtk * neg_a, cum_mat, precision=HIGHEST,
                                 preferred_element_type=F32)
        src = a_cst - LOG2_E * jnp.log(dtk)
        a_cs = jnp.concatenate([a_cst, jnp.zeros((q - heads, q), F32)], axis=0).T
        state_w = jnp.exp2(a_cst[:, q - 1:q] - a_cst) * dtk
        chunk_decay = jnp.exp2(a_cs[q - 1:q, :])
        for g in range(M2_GROUPS):
            b_g = xbc[:, d_inner + g * n:d_inner + (g + 1) * n]
            c_g = xbc[:, d_inner + M2_GROUPS * n + g * n:d_inner + M2_GROUPS * n + (g + 1) * n]
            cb = lax.dot_general(c_g.astype(BF16), b_g.astype(BF16), (((1,), (1,)), ((), ())),
                                 preferred_element_type=F32)
            b_t = b_g.T
            for j in range(pairs_per_group):
                h0 = g * heads_per_group + 2 * j
                sl = slice((h0 // 2) * hp, (h0 // 2 + 1) * hp)
                lhs, lhs_state = [], []
                for hd in (h0, h0 + 1):
                    bc = jnp.broadcast_to(a_cs[:, hd:hd + 1], (q, q))
                    decay_dt = jnp.exp2(jnp.where(causal, bc - src[hd:hd + 1, :], -jnp.inf))
                    lhs.append(cb * decay_dt)
                    lhs_state.append(b_t * state_w[hd:hd + 1, :])
                for hd in (h0, h0 + 1):
                    lhs.append(c_g * jnp.exp2(jnp.broadcast_to(a_cs[:, hd:hd + 1], (q, n))))
                xs_p = xbc[:, sl]
                st_p = st_ref[:, sl]
                x0 = jnp.where(first, xs_p, 0.0)
                x1 = jnp.where(first, 0.0, xs_p)
                s0 = jnp.where(first, st_p, 0.0)
                s1 = jnp.where(first, 0.0, st_p)
                lhs = jnp.concatenate(lhs, axis=1).astype(BF16)
                rhs = jnp.concatenate([x0, x1, s0, s1], axis=0).astype(BF16)
                y = jnp.dot(lhs, rhs, preferred_element_type=F32)
                y_ref[k, :, sl] = y + xs_p * dskip_ref[:, sl]
                s_new = jnp.dot(jnp.concatenate(lhs_state, axis=1).astype(BF16),
                                jnp.concatenate([x0, x1], axis=0).astype(BF16),
                                preferred_element_type=F32)
                cd = jnp.where(first_row, chunk_decay[:, h0:h0 + 1], chunk_decay[:, h0 + 1:h0 + 2])
                st_ref[:, sl] = st_p * cd + s_new

    def post(k):
        z = jnp.dot(h_ref[pl.ds(k * q, q), :], win_ref[:, :d_inner],
                    preferred_element_type=F32)
        y = y_ref[k] * (z * _sigmoid(z))
        gw = d_inner // M2_GROUPS
        parts = []
        for g in range(M2_GROUPS):
            yg = y[:, g * gw:(g + 1) * gw]
            parts.append(yg * lax.rsqrt(jnp.mean(yg * yg, axis=-1, keepdims=True) + NORM_EPS))
        y = jnp.concatenate(parts, axis=1) * ng_ref[...]
        out = jnp.dot(y.astype(BF16), wo_ref[...], preferred_element_type=F32)
        for ph in range(ph_n):
            res = (_strided_rows(tiles, k * q + ph, rpc, ph_n)
                   + gate_ref[0] * out[ph * rpc:(ph + 1) * rpc])
            _set_strided_rows(tiles, k * q + ph, rpc, ph_n, res)

    core(0)

    for k in range(1, kc):
        post(k - 1)
        core(k)
    post(kc - 1)
    _unstage_lane_tiles(tiles, o_ref)


def _ssd_layer(x, norm_g, shift, scale, gate, w_in, conv_w, conv_b, dt_bias, a_log,
               d_skip, gn_g, w_out):
    bsz, seq, d = x.shape
    d_inner = w_out.shape[0]
    heads = d_inner // M2_HEADDIM
    conv_dim = d_inner + 2 * M2_GROUPS * M2_STATE
    q = M2_CHUNK
    kc = M2_BLOCK_CHUNKS
    rows = kc * M2_PHASE_ROWS

    w_dt_t = w_in[:, d_inner + conv_dim:].T.astype(BF16)

    xv = x.reshape(bsz, seq // M2_PHASES, M2_PHASES, d)
    blk = pl.BlockSpec((1, rows, M2_PHASES, d), lambda b, i: (b, i, 0, 0))
    per_b = pl.BlockSpec((1, 1, d), lambda b, i: (b, 0, 0))
    out = pl.pallas_call(
        functools.partial(_ssd_kernel, d_inner=d_inner, heads=heads),
        grid=(bsz, seq // (kc * q)),
        in_specs=[blk, _const_spec((1, d)), per_b, per_b, per_b,
                  _const_spec(w_in.shape), _const_spec((heads, d)),
                  _const_spec((M2_CONV, conv_dim)), _const_spec((1, conv_dim)),
                  _const_spec((heads, 1)), _const_spec((heads, 1)),
                  _const_spec((1, d_inner)), _const_spec((1, d_inner)), _const_spec((d_inner, d))],
        out_specs=blk,
        out_shape=jax.ShapeDtypeStruct(xv.shape, F32),
        scratch_shapes=[pltpu.VMEM((kc * q, d), BF16),
                        pltpu.VMEM((kc, q, conv_dim), F32),
                        pltpu.VMEM((kc, heads, q), F32),
                        pltpu.VMEM((kc, q, d_inner), F32),
                        pltpu.VMEM((M2_CONV - 1, V7X_SUBLANES + rows, conv_dim), F32),
                        pltpu.VMEM((kc, (M2_CONV - 1) * M2_PHASE_ROWS, conv_dim), F32),
                        pltpu.VMEM((M2_STATE, d_inner), F32)]
        + [pltpu.VMEM((kc * q, V7X_LANES), F32)] * (d // V7X_LANES),
        compiler_params=_compiler_params(
            ("arbitrary", "arbitrary"),
            2 * _nbytes((kc * q, d), F32),
            _nbytes((d, 2 * d_inner + conv_dim), BF16)
            + 2 * _nbytes((kc * q, conv_dim + d_inner), F32)
            + 2 * _nbytes((kc * q, conv_dim), F32)),
        name="ssd_mixer",
    )(xv, norm_g.reshape(1, d), shift, scale, gate, w_in.astype(BF16), w_dt_t,
      conv_w, conv_b.reshape(1, conv_dim), dt_bias.reshape(heads, 1), a_log.reshape(heads, 1),
      jnp.repeat(d_skip, M2_HEADDIM).reshape(1, d_inner), gn_g.reshape(1, d_inner),
      w_out.astype(BF16))
    return out.reshape(bsz, seq, d)


def kernel(x, c, ada_w, ada_b, norm_mix_g, norm_mlp_g, mlp_w1, mlp_w2, s5_w_in, s5_lambda_re, s5_lambda_im, s5_log_dt, s5_b_re, s5_b_im, s5_c_re, s5_c_im, s5_d, s5_w_glu, s5_b_glu, m2_w_in, m2_conv_w, m2_conv_b, m2_dt_bias, m2_a_log, m2_d, m2_norm_g, m2_w_out, final_norm_g):
    depth = ada_w.shape[0]
    bsz, _, d = x.shape
    mod = _ada_modulation(c, ada_w, ada_b).reshape(depth, bsz, N_MOD, 1, d)
    w1_all, w2_all = mlp_w1.astype(BF16), mlp_w2.astype(BF16)
    for i in range(depth):
        sh1, sc1, g1, sh2, sc2, g2 = (mod[i, :, k] for k in range(N_MOD))
        j = i // 2
        if i % 2 == 0:
            x = _s5_layer(x, norm_mix_g[i], sh1, sc1, g1, s5_w_in[j], s5_lambda_re[j],
                          s5_lambda_im[j], s5_log_dt[j], s5_b_re[j], s5_b_im[j],
                          s5_c_re[j], s5_c_im[j], s5_d[j], s5_w_glu[j], s5_b_glu[j])
        else:
            x = _ssd_layer(x, norm_mix_g[i], sh1, sc1, g1, m2_w_in[j], m2_conv_w[j],
                           m2_conv_b[j], m2_dt_bias[j], m2_a_log[j], m2_d[j],
                           m2_norm_g[j], m2_w_out[j])
        x = _mlp_layer(x, norm_mlp_g[i], sh2, sc2, g2, w1_all, w2_all, i,
                       final_g=final_norm_g if i == depth - 1 else None)
    return x
```
